```python
import math
import jax, jax.numpy as jnp
from jax import lax
import numpy as np

D_MODEL = 1024
BATCH = 8
SEQ = 2048
DEPTH = 1
DEC_BATCH = 128
DEC_SEQ = 8
PAST_LEN = 16384
PAGE_SIZE = 128

D_MIX = 2 * D_MODEL
CONV_K = 4
CHUNK = 64
DN_HEADS = 8
DN_DK = 128
DN_DV = 128
DN_WIDTH = DN_HEADS * DN_DV
DN_QKV = 2 * DN_HEADS * DN_DK + DN_WIDTH
SSM_HEADDIM = 64
SSM_HEADS = (D_MIX - DN_WIDTH) // SSM_HEADDIM
SSM_GROUPS = 2
SSM_HPG = SSM_HEADS // SSM_GROUPS
SSM_STATE = 128
SSM_WIDTH = SSM_HEADS * SSM_HEADDIM
SSM_XBC = SSM_WIDTH + 2 * SSM_GROUPS * SSM_STATE
IN_SIZES = (DN_QKV, SSM_XBC, DN_WIDTH, SSM_WIDTH, DN_HEADS, DN_HEADS, SSM_HEADS)
D_IN_PROJ = DN_QKV + SSM_XBC + DN_WIDTH + SSM_WIDTH + 2 * DN_HEADS + SSM_HEADS
MEM_LEN = 256
MEM_HEADS = 4
MEM_HD = D_MODEL // MEM_HEADS
PEER_HEADS = 8
PEER_NKEYS = 128
PEER_EXPERTS = PEER_NKEYS * PEER_NKEYS
PEER_TOPK = 16
PEER_DKEY = 256
PEER_HALF = PEER_DKEY // 2
PEER_BLOCK = 256
EPS = 1e-6

kernel_name = 'hybrid_deltanet_ssd_peer_decoder'


def rmsnorm(x, w):
    xf = x.astype(jnp.float32)
    y = xf * lax.rsqrt(jnp.mean(xf * xf, axis=-1, keepdims=True) + EPS)
    return (y * w.astype(jnp.float32)).astype(x.dtype)


def l2norm(x):
    xf = x.astype(jnp.float32)
    return (xf * lax.rsqrt(jnp.sum(xf * xf, axis=-1, keepdims=True) + EPS)).astype(x.dtype)


def split_cols(t, sizes):
    out, o = [], 0
    for s in sizes:
        out.append(t[..., o:o + s])
        o += s
    return out


def causal_dwconv(x, buf, w, b=None):
    xp = jnp.concatenate([buf.astype(x.dtype), x], axis=1)
    c = x.shape[-1]
    y = lax.conv_general_dilated(xp, w[:, None, :], window_strides=(1,), padding='VALID',
                                 dimension_numbers=('NWC', 'WIO', 'NWC'), feature_group_count=c)
    if b is not None:
        y = y + b
    return y, xp[:, -(CONV_K - 1):]


def _pad_time(t, pad):
    return jnp.pad(t, [(0, 0), (0, pad)] + [(0, 0)] * (t.ndim - 2))


def _to_chunks(t, nc):
    b = t.shape[0]
    t = t.reshape((b, nc, CHUNK) + t.shape[2:])
    return t.transpose((1, 0, 3, 2) + tuple(range(4, t.ndim)))


def gated_delta_rule(q, k, v, beta, g, s0):
    f32 = jnp.float32
    b, l, h, _ = q.shape
    dv = v.shape[-1]
    nc = -(-l // CHUNK)
    pad = nc * CHUNK - l
    q, k, v, beta, g = [_to_chunks(_pad_time(t.astype(f32), pad), nc) for t in (q, k, v, beta, g)]
    gc = jnp.cumsum(g, axis=-1)
    incl = jnp.tril(jnp.ones((CHUNK, CHUNK), bool))
    strict = jnp.tril(jnp.ones((CHUNK, CHUNK), bool), -1)
    diff = gc[..., :, None] - gc[..., None, :]
    decay = jnp.where(incl, jnp.exp(jnp.where(incl, diff, 0.0)), 0.0)
    kb = k * beta[..., None]
    lower = jnp.where(strict, jnp.einsum('nbhid,nbhjd->nbhij', kb, k) * decay, 0.0)
    rhs = jnp.concatenate([v * beta[..., None], kb * jnp.exp(gc)[..., None]], axis=-1)
    uw = lax.linalg.triangular_solve(jnp.eye(CHUNK, dtype=f32) + lower, rhs,
                                     left_side=True, lower=True, unit_diagonal=True)
    u, w = uw[..., :dv], uw[..., dv:]
    attn = jnp.einsum('nbhid,nbhjd->nbhij', q, k) * decay
    q_dec = q * jnp.exp(gc)[..., None]
    k_dec = k * jnp.exp(gc[..., -1:] - gc)[..., None]
    g_tot = jnp.exp(gc[..., -1])

    def step(s, inp):
        u_c, w_c, qd_c, kd_c, a_c, gt_c = inp
        v_new = u_c - jnp.einsum('bhcd,bhde->bhce', w_c, s)
        o_c = jnp.einsum('bhcd,bhde->bhce', qd_c, s) + jnp.einsum('bhij,bhje->bhie', a_c, v_new)
        s = s * gt_c[..., None, None] + jnp.einsum('bhcd,bhce->bhde', kd_c, v_new)
        return s, o_c

    s, o = lax.scan(step, s0.astype(f32), (u, w, q_dec, k_dec, attn, g_tot))
    o = o.transpose(1, 0, 3, 2, 4).reshape(b, nc * CHUNK, h, dv)[:, :l]
    return o, s


def ssd_scan(x, dt, a_head, bm, cm, s0):
    f32 = jnp.float32
    b, l, ng, ne, p = x.shape
    n = bm.shape[-1]
    nc = -(-l // CHUNK)
    pad = nc * CHUNK - l
    dt = dt.astype(f32)
    xd = _pad_time(x.astype(f32) * dt[..., None], pad).reshape(b, nc, CHUNK, ng, ne, p)
    ac = _pad_time(dt * a_head, pad).reshape(b, nc, CHUNK, ng, ne).transpose(0, 1, 3, 4, 2)
    bc = _pad_time(bm.astype(f32), pad).reshape(b, nc, CHUNK, ng, n)
    cc = _pad_time(cm.astype(f32), pad).reshape(b, nc, CHUNK, ng, n)
    acs = jnp.cumsum(ac, axis=-1)
    incl = jnp.tril(jnp.ones((CHUNK, CHUNK), bool))
    diff = acs[..., :, None] - acs[..., None, :]
    lmat = jnp.where(incl, jnp.exp(jnp.where(incl, diff, 0.0)), 0.0)
    cb = jnp.einsum('bclgn,bcsgn->bcgls', cc, bc)
    y_diag = jnp.einsum('bcgls,bcgels,bcsgep->bclgep', cb, lmat, xd)
    ends = jnp.exp(acs[..., -1:] - acs)
    states = jnp.einsum('bclgn,bcgel,bclgep->bcgepn', bc, ends, xd)
    chunk_decay = jnp.exp(acs[..., -1])

    def step(hs, inp):
        st, dc = inp
        return hs * dc[..., None, None] + st, hs

    s, prev = lax.scan(step, s0.astype(f32), (jnp.moveaxis(states, 1, 0), jnp.moveaxis(chunk_decay, 1, 0)))
    prev = jnp.moveaxis(prev, 0, 1)
    y_off = jnp.einsum('bclgn,bcgepn,bcgel->bclgep', cc, prev, jnp.exp(acs))
    y = (y_diag + y_off).reshape(b, nc * CHUNK, ng, ne, p)[:, :l]
    return y, s


def memory_kv(mem, g_mem, w_mkv):
    b, m, _ = mem.shape
    mk, mv = split_cols(rmsnorm(mem, g_mem) @ w_mkv, (D_MODEL, D_MODEL))
    return mk.reshape(b, m, MEM_HEADS, MEM_HD), mv.reshape(b, m, MEM_HEADS, MEM_HD)


def memory_cross_attention(c, mem_k, mem_v, w_xq, w_xo):
    b, l, _ = c.shape
    q = (c @ w_xq).reshape(b, l, MEM_HEADS, MEM_HD)
    s = jnp.einsum('blhd,bmhd->bhlm', q, mem_k.astype(c.dtype)).astype(jnp.float32) * (MEM_HD ** -0.5)
    pr = jax.nn.softmax(s, axis=-1).astype(c.dtype)
    o = jnp.einsum('bhlm,bmhd->blhd', pr, mem_v.astype(c.dtype)).reshape(b, l, D_MODEL)
    return o @ w_xo


def peer_ffn(h, w_pq, sub_keys, peer_u, peer_v):
    bsz, l, d = h.shape
    n = bsz * l
    nb = -(-n // PEER_BLOCK)
    t = jnp.pad(h.reshape(n, d), ((0, nb * PEER_BLOCK - n), (0, 0))).reshape(nb, PEER_BLOCK, d)

    def block(tb):
        q = (tb @ w_pq).reshape(PEER_BLOCK, PEER_HEADS, 2, PEER_HALF)
        s = jnp.einsum('thcd,ckd->thck', q, sub_keys).astype(jnp.float32)
        s_top, i_top = lax.top_k(s, PEER_TOPK)
        cand_s = (s_top[:, :, 0, :, None] + s_top[:, :, 1, None, :]).reshape(PEER_BLOCK, PEER_HEADS, PEER_TOPK * PEER_TOPK)
        cand_i = (i_top[:, :, 0, :, None] * PEER_NKEYS + i_top[:, :, 1, None, :]).reshape(PEER_BLOCK, PEER_HEADS, PEER_TOPK * PEER_TOPK)
        best_s, pos = lax.top_k(cand_s, PEER_TOPK)
        idx = jnp.take_along_axis(cand_i, pos, axis=-1)
        gate = jax.nn.softmax(best_s, axis=-1).astype(tb.dtype)
        act = jax.nn.gelu(jnp.einsum('thkd,td->thk', peer_u[idx], tb), approximate=False)
        return jnp.einsum('thk,thkd->td', gate * act, peer_v[idx])

    out = lax.map(block, t)
    return out.reshape(nb * PEER_BLOCK, d)[:n].reshape(bsz, l, d)


def trunk_layer(x, mem_k, mem_v, dn_buf, dn_s, ssm_buf, ssm_s,
                g_mix, w_in, dn_conv_w, dn_A_log, dn_dt_bias, dn_norm_w,
                ssm_conv_w, ssm_conv_b, ssm_A_log, ssm_dt_bias, ssm_D, ssm_norm_w, w_out,
                g_xattn, w_xq, w_xo, g_ffn, w_pq, peer_sub_keys, peer_u, peer_v):
    f32 = jnp.float32
    bsz, l, _ = x.shape
    dtype = x.dtype
    proj = rmsnorm(x, g_mix) @ w_in
    qkv, xbc, z_dn, z_ssm, beta_raw, alpha_raw, dt_raw = split_cols(proj, IN_SIZES)
    qkv, dn_buf_new = causal_dwconv(qkv, dn_buf, dn_conv_w)
    q, k, v = split_cols(jax.nn.silu(qkv), (DN_HEADS * DN_DK, DN_HEADS * DN_DK, DN_WIDTH))
    q = l2norm(q.reshape(bsz, l, DN_HEADS, DN_DK)) * (DN_DK ** -0.5)
    k = l2norm(k.reshape(bsz, l, DN_HEADS, DN_DK))
    v = v.reshape(bsz, l, DN_HEADS, DN_DV)
    beta = jax.nn.sigmoid(beta_raw.astype(f32))
    g = -jnp.exp(dn_A_log.astype(f32)) * jax.nn.softplus(alpha_raw.astype(f32) + dn_dt_bias.astype(f32))
    o, dn_s_new = gated_delta_rule(q, k, v, beta, g, dn_s)
    o = rmsnorm(o.astype(dtype), dn_norm_w) * jax.nn.silu(z_dn.reshape(bsz, l, DN_HEADS, DN_DV))
    o_dn = o.reshape(bsz, l, DN_WIDTH)
    xbc, ssm_buf_new = causal_dwconv(xbc, ssm_buf, ssm_conv_w, ssm_conv_b)
    xs, bm, cm = split_cols(jax.nn.silu(xbc), (SSM_WIDTH, SSM_GROUPS * SSM_STATE, SSM_GROUPS * SSM_STATE))
    xs = xs.reshape(bsz, l, SSM_GROUPS, SSM_HPG, SSM_HEADDIM)
    bm = bm.reshape(bsz, l, SSM_GROUPS, SSM_STATE)
    cm = cm.reshape(bsz, l, SSM_GROUPS, SSM_STATE)
    dt = jax.nn.softplus(dt_raw.astype(f32) + ssm_dt_bias.astype(f32)).reshape(bsz, l, SSM_GROUPS, SSM_HPG)
    a_head = -jnp.exp(ssm_A_log.astype(f32)).reshape(SSM_GROUPS, SSM_HPG)
    y, ssm_s_new = ssd_scan(xs, dt, a_head, bm, cm,
                            ssm_s.reshape(bsz, SSM_GROUPS, SSM_HPG, SSM_HEADDIM, SSM_STATE))
    y = y + ssm_D.astype(f32).reshape(SSM_GROUPS, SSM_HPG)[..., None] * xs.astype(f32)
    y = y.astype(dtype) * jax.nn.silu(z_ssm.reshape(bsz, l, SSM_GROUPS, SSM_HPG, SSM_HEADDIM))
    y = rmsnorm(y.reshape(bsz, l, SSM_GROUPS, SSM_WIDTH // SSM_GROUPS), ssm_norm_w.reshape(SSM_GROUPS, -1))
    o_ssm = y.reshape(bsz, l, SSM_WIDTH)
    x = x + jnp.concatenate([o_dn, o_ssm], axis=-1) @ w_out
    x = x + memory_cross_attention(rmsnorm(x, g_xattn), mem_k, mem_v, w_xq, w_xo)
    x = x + peer_ffn(rmsnorm(x, g_ffn), w_pq, peer_sub_keys, peer_u, peer_v)
    ssm_s_out = ssm_s_new.reshape(bsz, SSM_HEADS, SSM_HEADDIM, SSM_STATE).astype(dtype)
    return x, dn_buf_new, dn_s_new.astype(dtype), ssm_buf_new, ssm_s_out


def setup_inputs(seed: int = 0) -> dict:
    key = jax.random.key(seed)
    ks = jax.random.split(key, 40)
    f32 = jnp.float32

    def nrm(i, shape, scale):
        return jax.random.normal(ks[i], shape, f32) * scale

    def gain(i, shape):
        return 1.0 + 0.05 * jax.random.normal(ks[i], shape, f32)

    def a_log(i, n):
        return jnp.log(jax.random.uniform(ks[i], (DEPTH, n), f32, 1.0, 16.0))

    def dt_bias(i, n):
        dt = jnp.exp(jax.random.uniform(ks[i], (DEPTH, n), f32, math.log(1e-3), math.log(1e-1)))
        return dt + jnp.log(-jnp.expm1(-dt))

    return {
        'x_prompt': nrm(0, (BATCH, SEQ, D_MODEL), 1.0),
        'x_sample': nrm(1, (DEC_BATCH, DEC_SEQ, D_MODEL), 1.0),
        'state_dn_conv': nrm(2, (DEPTH, DEC_BATCH, CONV_K - 1, DN_QKV), 1.0),
        'state_dn': nrm(3, (DEPTH, DEC_BATCH, DN_HEADS, DN_DK, DN_DV), 0.1),
        'state_ssm_conv': nrm(4, (DEPTH, DEC_BATCH, CONV_K - 1, SSM_XBC), 1.0),
        'state_ssm': nrm(5, (DEPTH, DEC_BATCH, SSM_HEADS, SSM_HEADDIM, SSM_STATE), 0.1),
        'cache_mem_k': nrm(6, (DEPTH, DEC_BATCH, MEM_LEN, MEM_HEADS, MEM_HD), 1.0),
        'cache_mem_v': nrm(7, (DEPTH, DEC_BATCH, MEM_LEN, MEM_HEADS, MEM_HD), 1.0),
        'mem_prompt': nrm(8, (BATCH, MEM_LEN, D_MODEL), 1.0),
        'g_mix': gain(9, (DEPTH, D_MODEL)),
        'w_in': nrm(10, (DEPTH, D_MODEL, D_IN_PROJ), D_MODEL ** -0.5),
        'dn_conv_w': nrm(11, (DEPTH, CONV_K, DN_QKV), CONV_K ** -0.5),
        'dn_A_log': a_log(12, DN_HEADS),
        'dn_dt_bias': dt_bias(13, DN_HEADS),
        'dn_norm_w': gain(14, (DEPTH, DN_DV)),
        'ssm_conv_w': nrm(15, (DEPTH, CONV_K, SSM_XBC), CONV_K ** -0.5),
        'ssm_conv_b': nrm(16, (DEPTH, SSM_XBC), 0.02),
        'ssm_A_log': a_log(17, SSM_HEADS),
        'ssm_dt_bias': dt_bias(18, SSM_HEADS),
        'ssm_D': gain(19, (DEPTH, SSM_HEADS)),
        'ssm_norm_w': gain(20, (DEPTH, SSM_WIDTH)),
        'w_out': nrm(21, (DEPTH, D_MIX, D_MODEL), D_MIX ** -0.5),
        'g_xattn': gain(22, (DEPTH, D_MODEL)),
        'g_mem': gain(23, (DEPTH, D_MODEL)),
        'w_xq': nrm(24, (DEPTH, D_MODEL, D_MODEL), D_MODEL ** -0.5),
        'w_mkv': nrm(25, (DEPTH, D_MODEL, 2 * D_MODEL), D_MODEL ** -0.5),
        'w_xo': nrm(26, (DEPTH, D_MODEL, D_MODEL), D_MODEL ** -0.5),
        'g_ffn': gain(27, (DEPTH, D_MODEL)),
        'w_pq': nrm(28, (DEPTH, D_MODEL, PEER_HEADS * PEER_DKEY), D_MODEL ** -0.5),
        'peer_sub_keys': nrm(29, (DEPTH, 2, PEER_NKEYS, PEER_HALF), PEER_HALF ** -0.5),
        'peer_u': nrm(30, (DEPTH, PEER_EXPERTS, D_MODEL), D_MODEL ** -0.5),
        'peer_v': nrm(31, (DEPTH, PEER_EXPERTS, D_MODEL), 0.25),
        'g_final': gain(32, (D_MODEL,)),
    }


def reference(x_prompt, x_sample, state_dn_conv, state_dn, state_ssm_conv, state_ssm,
              cache_mem_k, cache_mem_v, mem_prompt,
              g_mix, w_in, dn_conv_w, dn_A_log, dn_dt_bias, dn_norm_w,
              ssm_conv_w, ssm_conv_b, ssm_A_log, ssm_dt_bias, ssm_D, ssm_norm_w, w_out,
              g_xattn, g_mem, w_xq, w_mkv, w_xo, g_ffn, w_pq, peer_sub_keys, peer_u, peer_v, g_final):
    bp = x_prompt.shape[0]
    dtype = x_prompt.dtype
    hp, hs = x_prompt, x_sample
    p_dnc, p_dn, p_sc, p_ss, p_mk, p_mv = [], [], [], [], [], []
    s_dnc, s_dn, s_sc, s_ss = [], [], [], []
    for i in range(DEPTH):
        lp = (g_mix[i], w_in[i], dn_conv_w[i], dn_A_log[i], dn_dt_bias[i], dn_norm_w[i],
              ssm_conv_w[i], ssm_conv_b[i], ssm_A_log[i], ssm_dt_bias[i], ssm_D[i], ssm_norm_w[i], w_out[i],
              g_xattn[i], w_xq[i], w_xo[i], g_ffn[i], w_pq[i], peer_sub_keys[i], peer_u[i], peer_v[i])
        mk_p, mv_p = memory_kv(mem_prompt, g_mem[i], w_mkv[i])
        hp, dnc, dns, sc, ss = trunk_layer(
            hp, mk_p, mv_p,
            jnp.zeros((bp, CONV_K - 1, DN_QKV), dtype),
            jnp.zeros((bp, DN_HEADS, DN_DK, DN_DV), dtype),
            jnp.zeros((bp, CONV_K - 1, SSM_XBC), dtype),
            jnp.zeros((bp, SSM_HEADS, SSM_HEADDIM, SSM_STATE), dtype),
            *lp)
        p_dnc.append(dnc); p_dn.append(dns); p_sc.append(sc); p_ss.append(ss)
        p_mk.append(mk_p); p_mv.append(mv_p)
        hs, dnc, dns, sc, ss = trunk_layer(
            hs, cache_mem_k[i], cache_mem_v[i],
            state_dn_conv[i], state_dn[i], state_ssm_conv[i], state_ssm[i], *lp)
        s_dnc.append(dnc); s_dn.append(dns); s_sc.append(sc); s_ss.append(ss)
    y_prompt = rmsnorm(hp, g_final)
    y_sample = rmsnorm(hs, g_final)
    return (y_prompt, y_sample,
            jnp.stack(p_dnc), jnp.stack(p_dn), jnp.stack(p_sc), jnp.stack(p_ss),
            jnp.stack(p_mk), jnp.stack(p_mv),
            jnp.stack(s_dnc), jnp.stack(s_dn), jnp.stack(s_sc), jnp.stack(s_ss))
```

```python
import functools
import math

import jax
import jax.numpy as jnp
from jax import lax
from jax.experimental import pallas as pl
from jax.experimental.pallas import tpu as pltpu

F32 = jnp.float32
BF16 = jnp.bfloat16
EPS = 1e-6

D_MODEL = 1024
CONV_K = 4
CHUNK = 64
DN_HEADS = 8
DN_DK = 128
DN_WIDTH = 1024
DN_QKV = 3072
SSM_HEADS = 16
SSM_HEADDIM = 64
SSM_STATE = 128
SSM_WIDTH = 1024
SSM_XBC = 1536
MEM_LEN = 256
MEM_HEADS = 4
MEM_HD = 256
PEER_HEADS = 8
PEER_NKEYS = 128
PEER_EXPERTS = PEER_NKEYS * PEER_NKEYS
PEER_TOPK = 16
PEER_HALF = 128

COL_QKV = 0
COL_XBC = 3072
COL_GATE = 4608
COL_ZDN = 5120
COL_ZSSM = 6144
PROJ_COLS = 7168
GATE_W = 128
TAIL = 8
INV_BLOCK = 16
NOT_TOP = 99.0

VMEM_LIMIT = 56 * 1024 * 1024


def _cparams(sem):
    return pltpu.CompilerParams(dimension_semantics=sem, vmem_limit_bytes=VMEM_LIMIT)


def _bdot(a, b):
    return jnp.dot(a.astype(BF16), b.astype(BF16), preferred_element_type=F32)


def _bdot_nt(a, b):
    return lax.dot_general(a.astype(BF16), b.astype(BF16), (((1,), (1,)), ((), ())),
                           preferred_element_type=F32)


def _bdot_tn(a, b):
    return lax.dot_general(a.astype(BF16), b.astype(BF16), (((0,), (0,)), ((), ())),
                           preferred_element_type=F32)


def _split3(x):
    hi = x.astype(BF16)
    r = x - hi.astype(F32)
    mid = r.astype(BF16)
    lo = (r - mid.astype(F32)).astype(BF16)
    return hi, mid, lo


def _dot_sel_rhs(x, sel):
    hi, mid, lo = _split3(x)
    d = functools.partial(jnp.dot, preferred_element_type=F32)
    return d(hi, sel) + d(mid, sel) + d(lo, sel)


def _dot_sel_lhs(sel, x):
    hi, mid, lo = _split3(x)
    d = functools.partial(jnp.dot, preferred_element_type=F32)
    return d(sel, hi) + d(sel, mid) + d(sel, lo)


def _dot3(a, b):
    ah = a.astype(BF16)
    al = (a - ah.astype(F32)).astype(BF16)
    bh = b.astype(BF16)
    bl = (b - bh.astype(F32)).astype(BF16)
    d = functools.partial(jnp.dot, preferred_element_type=F32)
    return d(ah, bh) + d(ah, bl) + d(al, bh)


def _silu(x):
    return x * jax.nn.sigmoid(x)


def _softplus(x):
    return jnp.maximum(x, 0.0) + jnp.log1p(jnp.exp(-jnp.abs(x)))


def _unit_lower_inverse(lm, n):
    row = lax.broadcasted_iota(jnp.int32, (n, n), 0)
    col = lax.broadcasted_iota(jnp.int32, (n, n), 1)
    eye = (row == col).astype(F32)

    def nilpotent_inverse(x, index):
        inv = eye - x
        p = x
        k = 2
        while k < index:
            p = _dot3(p, p)
            inv = inv + _dot3(inv, p)
            k *= 2
        return inv

    if n <= INV_BLOCK:
        return nilpotent_inverse(lm, n)
    shift = INV_BLOCK.bit_length() - 1
    same = jnp.right_shift(row, shift) == jnp.right_shift(col, shift)
    d = jnp.where(same, lm, 0.0)
    e = jnp.where(same, 0.0, lm)
    dinv = nilpotent_inverse(d, INV_BLOCK)
    f = _dot3(dinv, e)
    finv = nilpotent_inverse(f, n // INV_BLOCK)
    return _dot3(finv, dinv)


def _causal_conv(x, xp_ref, w, lc):
    xp_ref[TAIL:TAIL + lc, :] = x
    y = x * w[3:4, :]
    for s in range(1, CONV_K):
        y = y + xp_ref[TAIL - s:TAIL - s + lc, :] * w[CONV_K - 1 - s:CONV_K - s, :]
    xp_ref[0:TAIL, :] = xp_ref[lc:lc + TAIL, :]
    return y


def _decay_matrix(col, row, lower_incl):
    diff = jnp.where(lower_incl, col - row, 0.0)
    return jnp.where(lower_incl, jnp.exp(diff), 0.0)


def _norm_matmul_kernel(x_ref, g_ref, w_ref, o_ref, xn_ref):
    @pl.when(pl.program_id(1) == 0)
    def _():
        x = x_ref[...]
        ms = jnp.mean(x * x, axis=-1, keepdims=True)
        xn_ref[...] = (x * lax.rsqrt(ms + EPS) * g_ref[...]).astype(BF16)

    o_ref[...] = jnp.dot(xn_ref[...], w_ref[...], preferred_element_type=F32).astype(o_ref.dtype)


def norm_matmul(x, g, w, *, tm, tn, out_dtype=F32):
    n, k = x.shape
    m = w.shape[1]
    assert n % tm == 0 and m % tn == 0
    return pl.pallas_call(
        _norm_matmul_kernel,
        grid=(n // tm, m // tn),
        in_specs=[pl.BlockSpec((tm, k), lambda i, j: (i, 0)),
                  pl.BlockSpec((1, k), lambda i, j: (0, 0)),
                  pl.BlockSpec((k, tn), lambda i, j: (0, j))],
        out_specs=pl.BlockSpec((tm, tn), lambda i, j: (i, j)),
        out_shape=jax.ShapeDtypeStruct((n, m), out_dtype),
        scratch_shapes=[pltpu.VMEM((tm, k), BF16)],
        compiler_params=_cparams(("parallel", "arbitrary")),
        name="norm_matmul",
    )(x, g.reshape(1, k), w)


def _matmul2_res_kernel(a1_ref, a2_ref, w1_ref, w2_ref, r_ref, o_ref):
    acc = jnp.dot(a1_ref[...].astype(BF16), w1_ref[...], preferred_element_type=F32)
    acc = acc + jnp.dot(a2_ref[...].astype(BF16), w2_ref[...], preferred_element_type=F32)
    o_ref[...] = r_ref[...] + acc


def matmul2_res(a1, a2, w1, w2, res, *, tm):
    n, k1 = a1.shape
    k2 = a2.shape[1]
    m = w1.shape[1]
    assert n % tm == 0
    return pl.pallas_call(
        _matmul2_res_kernel,
        grid=(n // tm,),
        in_specs=[pl.BlockSpec((tm, k1), lambda i: (i, 0)),
                  pl.BlockSpec((tm, k2), lambda i: (i, 0)),
                  pl.BlockSpec((k1, m), lambda i: (0, 0)),
                  pl.BlockSpec((k2, m), lambda i: (0, 0)),
                  pl.BlockSpec((tm, m), lambda i: (i, 0))],
        out_specs=pl.BlockSpec((tm, m), lambda i: (i, 0)),
        out_shape=jax.ShapeDtypeStruct((n, m), F32),
        compiler_params=_cparams(("parallel",)),
        name="matmul2_res",
    )(a1, a2, w1, w2, res)


def _matmul_res_kernel(a_ref, w_ref, r_ref, o_ref):
    o_ref[...] = r_ref[...] + jnp.dot(a_ref[...].astype(BF16), w_ref[...], preferred_element_type=F32)


def matmul_res(a, w, res, *, tm):
    n, k = a.shape
    m = w.shape[1]
    assert n % tm == 0
    return pl.pallas_call(
        _matmul_res_kernel,
        grid=(n // tm,),
        in_specs=[pl.BlockSpec((tm, k), lambda i: (i, 0)),
                  pl.BlockSpec((k, m), lambda i: (0, 0)),
                  pl.BlockSpec((tm, m), lambda i: (i, 0))],
        out_specs=pl.BlockSpec((tm, m), lambda i: (i, 0)),
        out_shape=jax.ShapeDtypeStruct((n, m), F32),
        compiler_params=_cparams(("parallel",)),
        name="matmul_res",
    )(a, w, res)


def _gdn_kernel(*refs, lc, has_state):
    if has_state:
        (qkv_ref, gate_ref, z_ref, buf_ref, s0_ref, cw_ref, gp_ref, nw_ref, repb_ref, repg_ref,
         o_ref, s_ref, xp_ref) = refs
    else:
        (qkv_ref, gate_ref, z_ref, cw_ref, gp_ref, nw_ref, repb_ref, repg_ref,
         o_ref, s_ref, xp_ref) = refs

    @pl.when(pl.program_id(1) == 0)
    def _():
        if has_state:
            xp_ref[0:TAIL, :] = buf_ref[0]
            s_ref[0] = s0_ref[0]
        else:
            xp_ref[0:TAIL, :] = jnp.zeros((TAIL, DN_QKV), F32)
            s_ref[0] = jnp.zeros((DN_HEADS, DN_DK, DN_DK), F32)

    qkv = _silu(_causal_conv(qkv_ref[0], xp_ref, cw_ref[...], lc))

    gate = gate_ref[0]
    beta_all = jax.nn.sigmoid(gate)
    g_all = -jnp.exp(gp_ref[0:1, :]) * _softplus(gate + gp_ref[1:2, :])
    row = lax.broadcasted_iota(jnp.int32, (lc, lc), 0)
    col = lax.broadcasted_iota(jnp.int32, (lc, lc), 1)
    incl = row >= col
    strict = row > col
    gc = _dot_sel_lhs(incl.astype(BF16), g_all)
    gc_t = jnp.transpose(gc)
    beta_b = _dot_sel_rhs(beta_all, repb_ref[...])
    gc_b = _dot_sel_rhs(gc, repg_ref[...])
    egc_b = jnp.exp(gc_b)
    gc_last_b = gc_b[lc - 1:lc, :]
    ekd_b = jnp.exp(gc_last_b - gc_b)
    egt_b = jnp.exp(gc_last_b)

    for h in range(DN_HEADS):
        sl = slice(h * DN_DK, (h + 1) * DN_DK)
        qh = qkv[:, h * DN_DK:(h + 1) * DN_DK]
        kh = qkv[:, DN_WIDTH + h * DN_DK:DN_WIDTH + (h + 1) * DN_DK]
        vh = qkv[:, 2 * DN_WIDTH + h * DN_DK:2 * DN_WIDTH + (h + 1) * DN_DK]
        qh = qh * lax.rsqrt(jnp.sum(qh * qh, axis=-1, keepdims=True) + EPS) * (DN_DK ** -0.5)
        kh = kh * lax.rsqrt(jnp.sum(kh * kh, axis=-1, keepdims=True) + EPS)
        bh = beta_b[:, sl]
        eg = egc_b[:, sl]
        decay = _decay_matrix(gc_b[:, h * DN_DK:h * DN_DK + lc],
                              gc_t[DN_HEADS + h:DN_HEADS + h + 1, :], incl)
        kb = kh * bh
        lower = jnp.where(strict, _bdot_nt(kb, kh) * decay, 0.0)
        tinv = _unit_lower_inverse(lower, lc)
        u = _bdot(tinv, vh * bh)
        w = _bdot(tinv, kb * eg)
        attn = _bdot_nt(qh, kh) * decay
        s = s_ref[0, h]
        v_new = u - _bdot(w, s)
        o = _bdot(qh * eg, s) + _bdot(attn, v_new)
        s_ref[0, h] = s * egt_b[:, sl] + _bdot_tn(kh * ekd_b[:, sl], v_new)
        o = o * lax.rsqrt(jnp.mean(o * o, axis=-1, keepdims=True) + EPS) * nw_ref[...]
        o_ref[0, :, sl] = o * _silu(z_ref[0, :, sl])


def gdn_mixer(proj3, row0, nseq, nchunk, lc, params, buf=None, s0=None):
    has_state = buf is not None
    cw, gp, nw, repb, repg = params

    def rb(b, c):
        return row0 + b * nchunk + c

    in_specs = [pl.BlockSpec((1, lc, DN_QKV), lambda b, c: (rb(b, c), 0, COL_QKV // DN_QKV)),
                pl.BlockSpec((1, lc, GATE_W), lambda b, c: (rb(b, c), 0, COL_GATE // GATE_W)),
                pl.BlockSpec((1, lc, DN_WIDTH), lambda b, c: (rb(b, c), 0, COL_ZDN // DN_WIDTH))]
    args = [proj3, proj3, proj3]
    if has_state:
        in_specs += [pl.BlockSpec((1, TAIL, DN_QKV), lambda b, c: (b, 0, 0)),
                     pl.BlockSpec((1, DN_HEADS, DN_DK, DN_DK), lambda b, c: (b, 0, 0, 0))]
        args += [buf, s0]
    in_specs += [pl.BlockSpec(cw.shape, lambda b, c: (0, 0)),
                 pl.BlockSpec(gp.shape, lambda b, c: (0, 0)),
                 pl.BlockSpec(nw.shape, lambda b, c: (0, 0)),
                 pl.BlockSpec(repb.shape, lambda b, c: (0, 0)),
                 pl.BlockSpec(repg.shape, lambda b, c: (0, 0))]
    args += [cw, gp, nw, repb, repg]
    return pl.pallas_call(
        functools.partial(_gdn_kernel, lc=lc, has_state=has_state),
        grid=(nseq, nchunk),
        in_specs=in_specs,
        out_specs=[pl.BlockSpec((1, lc, DN_WIDTH), lambda b, c: (b * nchunk + c, 0, 0)),
                   pl.BlockSpec((1, DN_HEADS, DN_DK, DN_DK), lambda b, c: (b, 0, 0, 0))],
        out_shape=[jax.ShapeDtypeStruct((nseq * nchunk, lc, DN_WIDTH), F32),
                   jax.ShapeDtypeStruct((nseq, DN_HEADS, DN_DK, DN_DK), F32)],
        scratch_shapes=[pltpu.VMEM((TAIL + lc, DN_QKV), F32)],
        compiler_params=_cparams(("parallel", "arbitrary")),
        name="gdn_mixer_state" if has_state else "gdn_mixer",
    )(*args)


SSM_PAIRS = SSM_HEADS // 2
PAIR_W = 2 * SSM_HEADDIM
GROUP_W = SSM_WIDTH // 2


def _ssd_kernel(*refs, lc, has_state):
    if has_state:
        (xbc_ref, gate_ref, z_ref, buf_ref, s0_ref, cw_ref, cb_ref, gp_ref, dd_ref, nw_ref,
         reps_ref, repw_ref, o_ref, s_ref, xp_ref) = refs
    else:
        (xbc_ref, gate_ref, z_ref, cw_ref, cb_ref, gp_ref, dd_ref, nw_ref,
         reps_ref, repw_ref, o_ref, s_ref, xp_ref) = refs

    @pl.when(pl.program_id(1) == 0)
    def _():
        if has_state:
            xp_ref[0:TAIL, :] = buf_ref[0]
            s_ref[0] = s0_ref[0]
        else:
            xp_ref[0:TAIL, :] = jnp.zeros((TAIL, SSM_XBC), F32)
            s_ref[0] = jnp.zeros((SSM_PAIRS, PAIR_W, SSM_STATE), F32)

    xbc = _silu(_causal_conv(xbc_ref[0], xp_ref, cw_ref[...], lc) + cb_ref[...])
    xs = xbc[:, :SSM_WIDTH]

    dt = _softplus(gate_ref[0] + gp_ref[1:2, :])
    a = dt * (-jnp.exp(gp_ref[0:1, :]))
    row = lax.broadcasted_iota(jnp.int32, (lc, lc), 0)
    col = lax.broadcasted_iota(jnp.int32, (lc, lc), 1)
    incl = row >= col
    acs = _dot_sel_lhs(incl.astype(BF16), a)
    acs_t = jnp.transpose(acs)
    dt_b = _dot_sel_rhs(dt, reps_ref[...])
    acs_b = _dot_sel_rhs(acs, reps_ref[...])
    acs_w = _dot_sel_rhs(acs, repw_ref[...])
    eacs_b = jnp.exp(acs_b)
    acs_last_b = acs_b[lc - 1:lc, :]
    ends_b = jnp.exp(acs_last_b - acs_b)
    ecd_w = jnp.exp(acs_w[lc - 1:lc, :])
    xd = xs * dt_b
    first_head = lax.broadcasted_iota(jnp.int32, (lc, PAIR_W), 1) < SSM_HEADDIM

    ys = []
    ssq = [None, None]
    for p in range(SSM_PAIRS):
        g = p // (SSM_PAIRS // 2)
        psl = slice(p * PAIR_W, (p + 1) * PAIR_W)
        bm = xbc[:, SSM_WIDTH + g * SSM_STATE:SSM_WIDTH + (g + 1) * SSM_STATE]
        cm = xbc[:, SSM_WIDTH + 2 * SSM_STATE + g * SSM_STATE:SSM_WIDTH + 2 * SSM_STATE + (g + 1) * SSM_STATE]
        cb = _bdot_nt(cm, bm)
        xd_p = xd[:, psl]
        y_heads = []
        for hh in (2 * p, 2 * p + 1):
            lmat = _decay_matrix(acs_w[:, hh * 128:hh * 128 + lc],
                                 acs_t[2 * DN_HEADS + hh:2 * DN_HEADS + hh + 1, :], incl)
            y_heads.append(_bdot(cb * lmat, xd_p))
        y = jnp.where(first_head, y_heads[0], y_heads[1])
        prev = s_ref[0, p]
        y = y + _bdot_nt(cm, prev) * eacs_b[:, psl]
        st = _bdot_tn(xd_p * ends_b[:, psl], bm)
        cd = jnp.concatenate(
            [jnp.broadcast_to(ecd_w[:, (2 * p) * 128:(2 * p + 1) * 128], (SSM_HEADDIM, SSM_STATE)),
             jnp.broadcast_to(ecd_w[:, (2 * p + 1) * 128:(2 * p + 2) * 128], (SSM_HEADDIM, SSM_STATE))],
            axis=0)
        s_ref[0, p] = prev * cd + st
        y = y + dd_ref[:, psl] * xs[:, psl]
        y = y * _silu(z_ref[0, :, psl])
        sq = jnp.sum(y * y, axis=-1, keepdims=True)
        ssq[g] = sq if ssq[g] is None else ssq[g] + sq
        ys.append(y)
    for p in range(SSM_PAIRS):
        g = p // (SSM_PAIRS // 2)
        psl = slice(p * PAIR_W, (p + 1) * PAIR_W)
        o_ref[0, :, psl] = ys[p] * lax.rsqrt(ssq[g] * (1.0 / GROUP_W) + EPS) * nw_ref[:, psl]


def ssd_mixer(proj3, row0, nseq, nchunk, lc, params, buf=None, s0=None):
    has_state = buf is not None
    cw, cb, gp, dd, nw, reps, repw = params

    def rb(b, c):
        return row0 + b * nchunk + c

    in_specs = [pl.BlockSpec((1, lc, SSM_XBC), lambda b, c: (rb(b, c), 0, COL_XBC // SSM_XBC)),
                pl.BlockSpec((1, lc, GATE_W), lambda b, c: (rb(b, c), 0, COL_GATE // GATE_W)),
                pl.BlockSpec((1, lc, SSM_WIDTH), lambda b, c: (rb(b, c), 0, COL_ZSSM // SSM_WIDTH))]
    args = [proj3, proj3, proj3]
    if has_state:
        in_specs += [pl.BlockSpec((1, TAIL, SSM_XBC), lambda b, c: (b, 0, 0)),
                     pl.BlockSpec((1, SSM_PAIRS, PAIR_W, SSM_STATE), lambda b, c: (b, 0, 0, 0))]
        args += [buf, s0]
    for prm in params:
        in_specs.append(pl.BlockSpec(prm.shape, lambda b, c: (0, 0)))
        args.append(prm)
    return pl.pallas_call(
        functools.partial(_ssd_kernel, lc=lc, has_state=has_state),
        grid=(nseq, nchunk),
        in_specs=in_specs,
        out_specs=[pl.BlockSpec((1, lc, SSM_WIDTH), lambda b, c: (b * nchunk + c, 0, 0)),
                   pl.BlockSpec((1, SSM_PAIRS, PAIR_W, SSM_STATE), lambda b, c: (b, 0, 0, 0))],
        out_shape=[jax.ShapeDtypeStruct((nseq * nchunk, lc, SSM_WIDTH), F32),
                   jax.ShapeDtypeStruct((nseq, SSM_PAIRS, PAIR_W, SSM_STATE), F32)],
        scratch_shapes=[pltpu.VMEM((TAIL + lc, SSM_XBC), F32)],
        compiler_params=_cparams(("parallel", "arbitrary")),
        name="ssd_mixer_state" if has_state else "ssd_mixer",
    )(*args)


def _xattn_kernel(q_ref, k_ref, v_ref, o_ref):
    for h in range(MEM_HEADS):
        sl = slice(h * MEM_HD, (h + 1) * MEM_HD)
        s = _bdot_nt(q_ref[0, :, sl], k_ref[0, :, sl]) * (MEM_HD ** -0.5)
        s = s - jnp.max(s, axis=-1, keepdims=True)
        p = jnp.exp(s)
        p = p / jnp.sum(p, axis=-1, keepdims=True)
        o_ref[0, :, sl] = _bdot(p, v_ref[0, :, sl])


def xattn_core(q3, row0, nseq, ntile, tl, mem_k, mem_v):
    return pl.pallas_call(
        _xattn_kernel,
        grid=(nseq, ntile),
        in_specs=[pl.BlockSpec((1, tl, D_MODEL), lambda b, t: (row0 + b * ntile + t, 0, 0)),
                  pl.BlockSpec((1, MEM_LEN, D_MODEL), lambda b, t: (b, 0, 0)),
                  pl.BlockSpec((1, MEM_LEN, D_MODEL), lambda b, t: (b, 0, 0))],
        out_specs=pl.BlockSpec((1, tl, D_MODEL), lambda b, t: (b * ntile + t, 0, 0)),
        out_shape=jax.ShapeDtypeStruct((nseq * ntile, tl, D_MODEL), F32),
        compiler_params=_cparams(("parallel", "arbitrary")),
        name="xattn_core",
    )(q3, mem_k, mem_v)


def _top16_rows(s):
    n = s.shape[0]
    iota = lax.broadcasted_iota(jnp.int32, s.shape, 0).astype(F32)
    rank = jnp.full(s.shape, NOT_TOP, F32)
    vals = []
    v = s
    for a in range(PEER_TOPK):
        m = jnp.max(v, axis=0, keepdims=True)
        idx = jnp.min(jnp.where(v == m, iota, float(n)), axis=0, keepdims=True)
        hit = iota == idx
        rank = jnp.where(hit, float(a), rank)
        v = jnp.where(hit, -jnp.inf, v)
        vals.append(m)
    return jnp.concatenate(vals, axis=0), rank


def _pair_top16(s0v, s1v):
    t = s0v.shape[1]
    iota = lax.broadcasted_iota(jnp.int32, (PEER_TOPK, t), 0).astype(F32)
    n = jnp.zeros((PEER_TOPK, t), F32)
    front = s0v + s1v[0:1, :]
    top = s0v[0:1, :] + s1v[0:1, :]
    z = jnp.zeros((1, t), F32)
    for _ in range(PEER_TOPK):
        m = jnp.max(front, axis=0, keepdims=True)
        a_star = jnp.min(jnp.where(front == m, iota, float(PEER_TOPK)), axis=0, keepdims=True)
        hit = iota == a_star
        z = z + jnp.exp(m - top)
        n = jnp.where(hit, n + 1.0, n)
        nxt = jnp.full((PEER_TOPK, t), -jnp.inf, F32)
        for b in range(1, PEER_TOPK):
            nxt = jnp.where(n == float(b), s1v[b:b + 1, :], nxt)
        front = jnp.where(hit, s0v + nxt, front)
    return n, z


def _peer_route_kernel(x_ref, g_ref, wq_ref, sk_ref, xn_ref, e0_ref, ni_ref, e1_ref, rj_ref):
    x = x_ref[...]
    ms = jnp.mean(x * x, axis=-1, keepdims=True)
    xn = (x * lax.rsqrt(ms + EPS) * g_ref[...]).astype(BF16)
    xn_ref[...] = xn
    q_t = lax.dot_general(wq_ref[...], xn, (((1,), (1,)), ((), ())),
                          preferred_element_type=F32)
    for h in range(PEER_HEADS):
        halves = []
        for c in range(2):
            r0 = (2 * h + c) * PEER_HALF
            s = _bdot(sk_ref[c], q_t[r0:r0 + PEER_HALF, :])
            vals, rank = _top16_rows(s)
            halves.append((s, vals, rank))
        (s0, s0v, rank0), (s1, s1v, rank1) = halves
        n, z = _pair_top16(s0v, s1v)
        in0 = rank0 < float(PEER_TOPK)
        in1 = rank1 < float(PEER_TOPK)
        e0 = jnp.where(in0, jnp.exp(jnp.where(in0, s0 - s0v[0:1, :], 0.0)), 0.0) / z
        e1 = jnp.where(in1, jnp.exp(jnp.where(in1, s1 - s1v[0:1, :], 0.0)), 0.0)
        ni = jnp.zeros_like(rank0)
        for a in range(PEER_TOPK):
            ni = jnp.where(rank0 == float(a), n[a:a + 1, :], ni)
        e0_ref[h] = e0
        ni_ref[h] = ni
        e1_ref[h] = e1
        rj_ref[h] = rank1


def peer_route(x, g, wq_t, sub_keys, *, tt):
    n = x.shape[0]
    fac = jax.ShapeDtypeStruct((PEER_HEADS, PEER_NKEYS, n), F32)
    fspec = pl.BlockSpec((PEER_HEADS, PEER_NKEYS, tt), lambda i: (0, 0, i))
    return pl.pallas_call(
        _peer_route_kernel,
        grid=(n // tt,),
        in_specs=[pl.BlockSpec((tt, D_MODEL), lambda i: (i, 0)),
                  pl.BlockSpec((1, D_MODEL), lambda i: (0, 0)),
                  pl.BlockSpec(wq_t.shape, lambda i: (0, 0)),
                  pl.BlockSpec(sub_keys.shape, lambda i: (0, 0, 0))],
        out_specs=[pl.BlockSpec((tt, D_MODEL), lambda i: (i, 0)), fspec, fspec, fspec, fspec],
        out_shape=[jax.ShapeDtypeStruct((n, D_MODEL), BF16), fac, fac, fac, fac],
        compiler_params=_cparams(("parallel",)),
        name="peer_route",
    )(x, g.reshape(1, D_MODEL), wq_t, sub_keys)


def _peer_dense_kernel(xn_ref, u_ref, vt_ref, e0_ref, ni_ref, e1_ref, rj_ref, x_ref, gf_ref,
                       y_ref, acc_ref, a_ref, p_ref, *, te):
    e = pl.program_id(1)

    @pl.when(e == 0)
    def _():
        acc_ref[...] = jnp.zeros_like(acc_ref)

    a_ref[...] = lax.dot_general(u_ref[...], xn_ref[...], (((1,), (1,)), ((), ())),
                                 preferred_element_type=F32)
    for ii in range(te // PEER_NKEYS):
        i = e * (te // PEER_NKEYS) + ii
        w = None
        for h in range(PEER_HEADS):
            e0r = e0_ref[h, pl.ds(i, 1), :]
            nir = ni_ref[h, pl.ds(i, 1), :]
            wh = jnp.where(rj_ref[h] < nir, e0r * e1_ref[h], 0.0)
            w = wh if w is None else w + wh
        a = a_ref[ii * PEER_NKEYS:(ii + 1) * PEER_NKEYS, :]
        act = 0.5 * a * (1.0 + lax.erf(a * (2.0 ** -0.5)))
        p_ref[ii * PEER_NKEYS:(ii + 1) * PEER_NKEYS, :] = (w * act).astype(BF16)
    acc_ref[...] += jnp.dot(vt_ref[...], p_ref[...], preferred_element_type=F32)

    @pl.when(e == pl.num_programs(1) - 1)
    def _():
        y = x_ref[...] + jnp.transpose(acc_ref[...])
        ms = jnp.mean(y * y, axis=-1, keepdims=True)
        y_ref[...] = y * lax.rsqrt(ms + EPS) * gf_ref[...]


def peer_dense(xn, u, v_t, e0, ni, e1, rj, x, g_final, *, tt, te):
    n = xn.shape[0]
    fspec = pl.BlockSpec((PEER_HEADS, PEER_NKEYS, tt), lambda i, e: (0, 0, i))
    return pl.pallas_call(
        functools.partial(_peer_dense_kernel, te=te),
        grid=(n // tt, PEER_EXPERTS // te),
        in_specs=[pl.BlockSpec((tt, D_MODEL), lambda i, e: (i, 0)),
                  pl.BlockSpec((te, D_MODEL), lambda i, e: (e, 0)),
                  pl.BlockSpec((D_MODEL, te), lambda i, e: (0, e)),
                  fspec, fspec, fspec, fspec,
                  pl.BlockSpec((tt, D_MODEL), lambda i, e: (i, 0)),
                  pl.BlockSpec((1, D_MODEL), lambda i, e: (0, 0))],
        out_specs=pl.BlockSpec((tt, D_MODEL), lambda i, e: (i, 0)),
        out_shape=jax.ShapeDtypeStruct((n, D_MODEL), F32),
        scratch_shapes=[pltpu.VMEM((D_MODEL, tt), F32),
                        pltpu.VMEM((te, tt), F32),
                        pltpu.VMEM((te, tt), BF16)],
        compiler_params=_cparams(("parallel", "arbitrary")),
        name="peer_dense",
    )(xn, u, v_t, e0, ni, e1, rj, x, g_final.reshape(1, D_MODEL))


def _lane_row(vec, start, width=GATE_W):
    return jnp.zeros((1, width), F32).at[0, start:start + vec.shape[0]].set(vec.astype(F32))


def _rep_matrix(first_row, heads, lanes_per_head, rows=GATE_W):
    r = jnp.arange(rows)[:, None]
    c = jnp.arange(heads * lanes_per_head)[None, :]
    return ((c // lanes_per_head) + first_row == r).astype(BF16)


def _pad_tail(buf):
    return jnp.pad(buf, ((0, 0), (TAIL - (CONV_K - 1), 0), (0, 0)))


def kernel(x_prompt, x_sample, state_dn_conv, state_dn, state_ssm_conv, state_ssm, cache_mem_k, cache_mem_v,
           mem_prompt, g_mix, w_in, dn_conv_w, dn_A_log, dn_dt_bias, dn_norm_w, ssm_conv_w, ssm_conv_b,
           ssm_A_log, ssm_dt_bias, ssm_D, ssm_norm_w, w_out, g_xattn, g_mem, w_xq, w_mkv, w_xo, g_ffn, w_pq,
           peer_sub_keys, peer_u, peer_v, g_final):
    depth = g_mix.shape[0]
    assert depth == 1
    bp, lp, d = x_prompt.shape
    bs, ls, _ = x_sample.shape
    n_p, n_s = bp * lp, bs * ls
    n = n_p + n_s
    ncp = lp // CHUNK

    x = jnp.concatenate([x_prompt.reshape(n_p, d), x_sample.reshape(n_s, d)], axis=0)

    wi = w_in[0]
    o_qkv, o_xbc, o_zdn, o_zssm, o_small = 0, DN_QKV, DN_QKV + SSM_XBC, DN_QKV + SSM_XBC + DN_WIDTH, \
        DN_QKV + SSM_XBC + DN_WIDTH + SSM_WIDTH
    n_small = 2 * DN_HEADS + SSM_HEADS
    w_cat = jnp.concatenate([
        wi[:, o_qkv:o_qkv + DN_QKV], wi[:, o_xbc:o_xbc + SSM_XBC],
        wi[:, o_small:o_small + n_small], jnp.zeros((d, COL_ZDN - COL_GATE - n_small), F32),
        wi[:, o_zdn:o_zdn + DN_WIDTH], wi[:, o_zssm:o_zssm + SSM_WIDTH]], axis=1).astype(BF16)

    proj = norm_matmul(x, g_mix[0], w_cat, tm=512, tn=1024)
    proj_c = proj.reshape(n // CHUNK, CHUNK, PROJ_COLS)
    proj_s = proj.reshape(n // ls, ls, PROJ_COLS)

    gdn_params = (dn_conv_w[0],
                  jnp.concatenate([_lane_row(dn_A_log[0], DN_HEADS), _lane_row(dn_dt_bias[0], DN_HEADS)], axis=0),
                  dn_norm_w[0].reshape(1, DN_DK),
                  _rep_matrix(0, DN_HEADS, DN_DK), _rep_matrix(DN_HEADS, DN_HEADS, DN_DK))
    ssd_params = (ssm_conv_w[0], ssm_conv_b[0].reshape(1, SSM_XBC),
                  jnp.concatenate([_lane_row(ssm_A_log[0], 2 * DN_HEADS), _lane_row(ssm_dt_bias[0], 2 * DN_HEADS)],
                                  axis=0),
                  jnp.repeat(ssm_D[0], SSM_HEADDIM).reshape(1, SSM_WIDTH),
                  ssm_norm_w[0].reshape(1, SSM_WIDTH),
                  _rep_matrix(2 * DN_HEADS, SSM_HEADS, SSM_HEADDIM), _rep_matrix(2 * DN_HEADS, SSM_HEADS, 128))

    odn_p, p_dn = gdn_mixer(proj_c, 0, bp, ncp, CHUNK, gdn_params)
    odn_s, s_dn = gdn_mixer(proj_s, n_p // ls, bs, 1, ls, gdn_params,
                            buf=_pad_tail(state_dn_conv[0]), s0=state_dn[0])
    ossm_p, p_ss = ssd_mixer(proj_c, 0, bp, ncp, CHUNK, ssd_params)
    ossm_s, s_ss = ssd_mixer(proj_s, n_p // ls, bs, 1, ls, ssd_params,
                             buf=_pad_tail(state_ssm_conv[0]),
                             s0=state_ssm[0].reshape(bs, SSM_PAIRS, PAIR_W, SSM_STATE))
    o_dn = jnp.concatenate([odn_p.reshape(n_p, DN_WIDTH), odn_s.reshape(n_s, DN_WIDTH)], axis=0)
    o_ssm = jnp.concatenate([ossm_p.reshape(n_p, SSM_WIDTH), ossm_s.reshape(n_s, SSM_WIDTH)], axis=0)
    wo = w_out[0].astype(BF16)
    x1 = matmul2_res(o_dn, o_ssm, wo[:DN_WIDTH], wo[DN_WIDTH:], x, tm=512)

    proj_p4 = proj[:n_p].reshape(bp, lp, PROJ_COLS)
    proj_s4 = proj[n_p:].reshape(bs, ls, PROJ_COLS)
    p_dnc = proj_p4[:, lp - 3:, COL_QKV:COL_QKV + DN_QKV]
    p_sc = proj_p4[:, lp - 3:, COL_XBC:COL_XBC + SSM_XBC]
    s_dnc = proj_s4[:, ls - 3:, COL_QKV:COL_QKV + DN_QKV]
    s_sc = proj_s4[:, ls - 3:, COL_XBC:COL_XBC + SSM_XBC]

    mkv = norm_matmul(mem_prompt.reshape(bp * MEM_LEN, d), g_mem[0], w_mkv[0].astype(BF16), tm=512, tn=1024)
    mk_p = mkv[:, :d].reshape(bp, MEM_LEN, d)
    mv_p = mkv[:, d:].reshape(bp, MEM_LEN, d)
    q = norm_matmul(x1, g_xattn[0], w_xq[0].astype(BF16), tm=512, tn=1024)
    tl = 512
    a_p = xattn_core(q.reshape(n // tl, tl, d), 0, bp, lp // tl, tl, mk_p, mv_p)
    a_s = xattn_core(q.reshape(n // ls, ls, d), n_p // ls, bs, 1, ls,
                     cache_mem_k[0].reshape(bs, MEM_LEN, d), cache_mem_v[0].reshape(bs, MEM_LEN, d))
    att = jnp.concatenate([a_p.reshape(n_p, d), a_s.reshape(n_s, d)], axis=0)
    x2 = matmul_res(att, w_xo[0].astype(BF16), x1, tm=512)

    tt = 512
    xn, e0, ni, e1, rj = peer_route(x2, g_ffn[0], jnp.transpose(w_pq[0]).astype(BF16),
                                    peer_sub_keys[0].astype(BF16), tt=256)
    y = peer_dense(xn, peer_u[0].astype(BF16), jnp.transpose(peer_v[0]).astype(BF16),
                   e0, ni, e1, rj, x2, g_final, tt=tt, te=1024)

    y_prompt = y[:n_p].reshape(bp, lp, d)
    y_sample = y[n_p:].reshape(bs, ls, d)
    return (y_prompt, y_sample,
            p_dnc[None], p_dn[None], p_sc[None], p_ss.reshape(bp, SSM_HEADS, SSM_HEADDIM, SSM_STATE)[None],
            mk_p.reshape(bp, MEM_LEN, MEM_HEADS, MEM_HD)[None], mv_p.reshape(bp, MEM_LEN, MEM_HEADS, MEM_HD)[None],
            s_dnc[None], s_dn[None], s_sc[None], s_ss.reshape(bs, SSM_HEADS, SSM_HEADDIM, SSM_STATE)[None])
```

```python
import functools

import jax
import jax.numpy as jnp
from jax import lax
from jax.experimental import pallas as pl
from jax.experimental.pallas import tpu as pltpu

F32 = jnp.float32
BF16 = jnp.bfloat16
EPS = 1e-6

D_MODEL = 1024
CONV_K = 4
CHUNK = 64
DN_HEADS = 8
DN_DK = 128
DN_WIDTH = 1024
DN_QKV = 3072
SSM_HEADS = 16
SSM_HEADDIM = 64
SSM_STATE = 128
SSM_WIDTH = 1024
SSM_XBC = 1536
MEM_LEN = 256
MEM_HEADS = 4
MEM_HD = 256
PEER_HEADS = 8
PEER_NKEYS = 128
PEER_EXPERTS = PEER_NKEYS * PEER_NKEYS
PEER_TOPK = 16
PEER_HALF = 128

COL_QKV = 0
COL_XBC = 3072
COL_GATE = 4608
COL_ZDN = 5120
COL_ZSSM = 6144
PROJ_COLS = 7168
GATE_W = 128
TAIL = 8
INV_BLOCK = 16
NOT_TOP = 99.0
LANES = 128

VMEM_LIMIT = 56 * 1024 * 1024


def _cparams(sem):
    return pltpu.CompilerParams(dimension_semantics=sem, vmem_limit_bytes=VMEM_LIMIT)


def _bdot(a, b):
    return jnp.dot(a.astype(BF16), b.astype(BF16), preferred_element_type=F32)


def _bdot_nt(a, b):
    return lax.dot_general(a.astype(BF16), b.astype(BF16), (((1,), (1,)), ((), ())),
                           preferred_element_type=F32)


def _bdot_tn(a, b):
    return lax.dot_general(a.astype(BF16), b.astype(BF16), (((0,), (0,)), ((), ())),
                           preferred_element_type=F32)


def _split3(x):
    hi = x.astype(BF16)
    r = x - hi.astype(F32)
    mid = r.astype(BF16)
    lo = (r - mid.astype(F32)).astype(BF16)
    return hi, mid, lo


def _dot_sel_rhs(x, sel):
    hi, mid, lo = _split3(x)
    d = functools.partial(jnp.dot, preferred_element_type=F32)
    return d(hi, sel) + d(mid, sel) + d(lo, sel)


def _dot_sel_lhs(sel, x):
    hi, mid, lo = _split3(x)
    d = functools.partial(jnp.dot, preferred_element_type=F32)
    return d(sel, hi) + d(sel, mid) + d(sel, lo)


def _dot3(a, b):
    ah = a.astype(BF16)
    al = (a - ah.astype(F32)).astype(BF16)
    bh = b.astype(BF16)
    bl = (b - bh.astype(F32)).astype(BF16)
    d = functools.partial(jnp.dot, preferred_element_type=F32)
    return d(ah, bh) + d(ah, bl) + d(al, bh)


def _silu(x):
    return x * jax.nn.sigmoid(x)


def _softplus(x):
    return jnp.maximum(x, 0.0) + jnp.log1p(jnp.exp(-jnp.abs(x)))


def _unit_lower_inverse(lms, n):
    row = lax.broadcasted_iota(jnp.int32, (n, n), 0)
    col = lax.broadcasted_iota(jnp.int32, (n, n), 1)
    eye = (row == col).astype(F32)

    def nilpotent_inverse(xs, index):
        invs = [eye - x for x in xs]
        ps = xs
        k = 2
        while k < index:
            ps = [_dot3(p, p) for p in ps]
            invs = [inv + _dot3(inv, p) for inv, p in zip(invs, ps)]
            k *= 2
        return invs

    if n <= INV_BLOCK:
        return nilpotent_inverse(lms, n)
    shift = INV_BLOCK.bit_length() - 1
    same = jnp.right_shift(row, shift) == jnp.right_shift(col, shift)
    dinvs = nilpotent_inverse([jnp.where(same, lm, 0.0) for lm in lms], INV_BLOCK)
    fs = [_dot3(dinv, jnp.where(same, 0.0, lm)) for dinv, lm in zip(dinvs, lms)]
    finvs = nilpotent_inverse(fs, n // INV_BLOCK)
    return [_dot3(finv, dinv) for finv, dinv in zip(finvs, dinvs)]


def _causal_conv(x, xp_ref, w, lc):
    xp_ref[TAIL:TAIL + lc, :] = x
    y = x * w[3:4, :]
    for s in range(1, CONV_K):
        y = y + xp_ref[TAIL - s:TAIL - s + lc, :] * w[CONV_K - 1 - s:CONV_K - s, :]
    xp_ref[0:TAIL, :] = xp_ref[lc:lc + TAIL, :]
    return y


def _decay_matrix(col, row, lower_incl):
    diff = jnp.where(lower_incl, col - row, 0.0)
    return jnp.where(lower_incl, jnp.exp(diff), 0.0)


def _pair_specs(tm, k, np_tiles, two_axes):
    if two_axes:
        return [pl.BlockSpec((tm, k), lambda i, j: (jnp.minimum(i, np_tiles - 1), 0)),
                pl.BlockSpec((tm, k), lambda i, j: (jnp.maximum(i - np_tiles, 0), 0))]
    return [pl.BlockSpec((tm, k), lambda i: (jnp.minimum(i, np_tiles - 1), 0)),
            pl.BlockSpec((tm, k), lambda i: (jnp.maximum(i - np_tiles, 0), 0))]


def _pick(is_prompt, p_ref, s_ref):
    return jnp.where(is_prompt, p_ref[...], s_ref[...])


def _rmsnorm_rows(x, g):
    ms = jnp.mean(x * x, axis=-1, keepdims=True)
    return x * lax.rsqrt(ms + EPS) * g


def _norm_matmul_kernel(*refs, np_tiles):
    if np_tiles is None:
        x_ref, g_ref, w_ref, o_ref, xn_ref = refs
    else:
        xp_ref, xs_ref, g_ref, w_ref, o_ref, xn_ref = refs
        is_prompt = pl.program_id(0) < np_tiles

    @pl.when(pl.program_id(1) == 0)
    def _():
        x = x_ref[...] if np_tiles is None else _pick(is_prompt, xp_ref, xs_ref)
        xn_ref[...] = _rmsnorm_rows(x, g_ref[...]).astype(BF16)

    o_ref[...] = jnp.dot(xn_ref[...], w_ref[...], preferred_element_type=F32).astype(o_ref.dtype)


def norm_matmul(x, g, w, *, tm, tn, out_dtype=F32):
    pair = isinstance(x, tuple)
    n = sum(a.shape[0] for a in x) if pair else x.shape[0]
    k, m = w.shape
    assert n % tm == 0 and m % tn == 0
    if pair:
        assert x[0].shape[0] % tm == 0
        np_tiles = x[0].shape[0] // tm
        x_specs = _pair_specs(tm, k, np_tiles, True)
        x_args = list(x)
    else:
        np_tiles = None
        x_specs = [pl.BlockSpec((tm, k), lambda i, j: (i, 0))]
        x_args = [x]
    return pl.pallas_call(
        functools.partial(_norm_matmul_kernel, np_tiles=np_tiles),
        grid=(n // tm, m // tn),
        in_specs=x_specs + [pl.BlockSpec((1, k), lambda i, j: (0, 0)),
                            pl.BlockSpec((k, tn), lambda i, j: (0, j))],
        out_specs=pl.BlockSpec((tm, tn), lambda i, j: (i, j)),
        out_shape=jax.ShapeDtypeStruct((n, m), out_dtype),
        scratch_shapes=[pltpu.VMEM((tm, k), BF16)],
        compiler_params=_cparams(("arbitrary", "arbitrary")),
        name="norm_matmul",
    )(*x_args, g.reshape(1, k), w)


def _mixer_out_kernel(dp_ref, ds_ref, sp_ref, ss_ref, w1_ref, w2_ref, xp_ref, xs_ref, o_ref, *, np_tiles):
    is_prompt = pl.program_id(0) < np_tiles
    a1 = _pick(is_prompt, dp_ref, ds_ref).astype(BF16)
    a2 = _pick(is_prompt, sp_ref, ss_ref).astype(BF16)
    acc = jnp.dot(a1, w1_ref[...], preferred_element_type=F32)
    acc = acc + jnp.dot(a2, w2_ref[...], preferred_element_type=F32)
    o_ref[...] = _pick(is_prompt, xp_ref, xs_ref) + acc


def mixer_out(o_dn, o_ssm, w1, w2, x, *, tm):
    n = x[0].shape[0] + x[1].shape[0]
    k = w1.shape[0]
    m = w1.shape[1]
    np_tiles = x[0].shape[0] // tm
    wspec = pl.BlockSpec((k, m), lambda i: (0, 0))
    return pl.pallas_call(
        functools.partial(_mixer_out_kernel, np_tiles=np_tiles),
        grid=(n // tm,),
        in_specs=(_pair_specs(tm, k, np_tiles, False) + _pair_specs(tm, k, np_tiles, False)
                  + [wspec, wspec] + _pair_specs(tm, m, np_tiles, False)),
        out_specs=pl.BlockSpec((tm, m), lambda i: (i, 0)),
        out_shape=jax.ShapeDtypeStruct((n, m), F32),
        compiler_params=_cparams(("arbitrary",)),
        name="mixer_out",
    )(*o_dn, *o_ssm, w1, w2, *x)


def _matmul_res_kernel(ap_ref, as_ref, w_ref, r_ref, o_ref, *, np_tiles):
    a = _pick(pl.program_id(0) < np_tiles, ap_ref, as_ref).astype(BF16)
    o_ref[...] = r_ref[...] + jnp.dot(a, w_ref[...], preferred_element_type=F32)


def matmul_res(a, w, res, *, tm):
    n, m = res.shape
    k = w.shape[0]
    np_tiles = a[0].shape[0] // tm
    return pl.pallas_call(
        functools.partial(_matmul_res_kernel, np_tiles=np_tiles),
        grid=(n // tm,),
        in_specs=_pair_specs(tm, k, np_tiles, False) + [pl.BlockSpec((k, m), lambda i: (0, 0)),
                                                        pl.BlockSpec((tm, m), lambda i: (i, 0))],
        out_specs=pl.BlockSpec((tm, m), lambda i: (i, 0)),
        out_shape=jax.ShapeDtypeStruct((n, m), F32),
        compiler_params=_cparams(("arbitrary",)),
        name="matmul_res",
    )(*a, w, res)


def _gdn_kernel(*refs, lc, ng, nbk, has_state):
    refs = list(refs)

    def take(k):
        out = refs[:k]
        del refs[:k]
        return out

    qkv_refs, gate_refs, z_refs = take(ng), take(ng), take(ng)
    if has_state:
        buf_ref, s0_ref = take(2)
    cw_ref, gp_ref, nw_ref, repb_ref, repg_ref, o_ref, s_ref, xp_ref = take(8)
    nseq = ng * nbk
    where = [divmod(j, nbk) for j in range(nseq)]

    @pl.when(pl.program_id(1) == 0)
    def _():
        if has_state:
            xp_ref[:, 0:TAIL, :] = buf_ref[...]
            s_ref[...] = s0_ref[...]
        else:
            xp_ref[:, 0:TAIL, :] = jnp.zeros((nseq, TAIL, DN_QKV), F32)
            s_ref[...] = jnp.zeros(s_ref.shape, F32)

    row = lax.broadcasted_iota(jnp.int32, (lc, lc), 0)
    col = lax.broadcasted_iota(jnp.int32, (lc, lc), 1)
    incl = row >= col
    strict = row > col
    tri = incl.astype(BF16)

    seqs = []
    for j, (g_, k_) in enumerate(where):
        qkv = _silu(_causal_conv(qkv_refs[g_][k_], xp_ref.at[j], cw_ref[...], lc))
        gate = gate_refs[g_][k_]
        beta_all = jax.nn.sigmoid(gate)
        g_all = -jnp.exp(gp_ref[0:1, :]) * _softplus(gate + gp_ref[1:2, :])
        gc = _dot_sel_lhs(tri, g_all)
        gc_b = _dot_sel_rhs(gc, repg_ref[...])
        gc_last_b = gc_b[lc - 1:lc, :]
        seqs.append(dict(qkv=qkv, gc_t=jnp.transpose(gc), gc_b=gc_b,
                         beta_b=_dot_sel_rhs(beta_all, repb_ref[...]),
                         egc_b=jnp.exp(gc_b), ekd_b=jnp.exp(gc_last_b - gc_b), egt_b=jnp.exp(gc_last_b)))

    units = [(j, h) for j in range(nseq) for h in range(DN_HEADS)]
    qs, ks, vbs, kbegs, decays = [], [], [], [], []
    for j, h in units:
        sq = seqs[j]
        sl = slice(h * DN_DK, (h + 1) * DN_DK)
        qh = sq["qkv"][:, h * DN_DK:(h + 1) * DN_DK]
        kh = sq["qkv"][:, DN_WIDTH + h * DN_DK:DN_WIDTH + (h + 1) * DN_DK]
        vh = sq["qkv"][:, 2 * DN_WIDTH + h * DN_DK:2 * DN_WIDTH + (h + 1) * DN_DK]
        qh = qh * lax.rsqrt(jnp.sum(qh * qh, axis=-1, keepdims=True) + EPS) * (DN_DK ** -0.5)
        kh = kh * lax.rsqrt(jnp.sum(kh * kh, axis=-1, keepdims=True) + EPS)
        bh = sq["beta_b"][:, sl]
        qs.append(qh)
        ks.append(kh)
        vbs.append(vh * bh)
        kbegs.append((kh * bh, sq["egc_b"][:, sl]))
        decays.append(_decay_matrix(sq["gc_b"][:, h * DN_DK:h * DN_DK + lc],
                                    sq["gc_t"][DN_HEADS + h:DN_HEADS + h + 1, :], incl))
    lowers = [jnp.where(strict, _bdot_nt(kb, kh) * dec, 0.0)
              for (kb, _), kh, dec in zip(kbegs, ks, decays)]
    attns = [_bdot_nt(qh, kh) * dec for qh, kh, dec in zip(qs, ks, decays)]
    tinvs = _unit_lower_inverse(lowers, lc)
    us = [_bdot(t, vb) for t, vb in zip(tinvs, vbs)]
    ws = [_bdot(t, kb * eg) for t, (kb, eg) in zip(tinvs, kbegs)]
    states = [s_ref[j, h] for j, h in units]
    v_news = [u - _bdot(w, s) for u, w, s in zip(us, ws, states)]
    os_ = [_bdot(qh * eg, s) + _bdot(a, vn)
           for qh, (_, eg), s, a, vn in zip(qs, kbegs, states, attns, v_news)]
    for (j, h), kh, s, vn, o in zip(units, ks, states, v_news, os_):
        g_, k_ = where[j]
        sl = slice(h * DN_DK, (h + 1) * DN_DK)
        sq = seqs[j]
        s_ref[j, h] = s * sq["egt_b"][:, sl] + _bdot_tn(kh * sq["ekd_b"][:, sl], vn)
        o = o * lax.rsqrt(jnp.mean(o * o, axis=-1, keepdims=True) + EPS) * nw_ref[...]
        o_ref[j, 0, :, sl] = o * _silu(z_refs[g_][k_, :, sl])


def gdn_mixer(proj3, row0, nseq, nchunk, lc, ng, nbk, params, buf=None, s0=None):
    has_state = buf is not None
    step = ng * nbk
    assert nseq % step == 0 and row0 % nbk == 0 and (nbk == 1 or nchunk == 1)

    def proj_specs(width, col_block):
        return [pl.BlockSpec((nbk, lc, width),
                             lambda b, c, g_=g_: (row0 // nbk + (b * ng + g_) * nchunk + c, 0, col_block))
                for g_ in range(ng)]

    in_specs = (proj_specs(DN_QKV, COL_QKV // DN_QKV) + proj_specs(GATE_W, COL_GATE // GATE_W)
                + proj_specs(DN_WIDTH, COL_ZDN // DN_WIDTH))
    args = [proj3] * (3 * ng)
    state_spec = pl.BlockSpec((step, DN_HEADS, DN_DK, DN_DK), lambda b, c: (b, 0, 0, 0))
    if has_state:
        in_specs += [pl.BlockSpec((step, TAIL, DN_QKV), lambda b, c: (b, 0, 0)), state_spec]
        args += [buf, s0]
    for prm in params:
        in_specs.append(pl.BlockSpec(prm.shape, lambda b, c: (0, 0)))
        args.append(prm)
    return pl.pallas_call(
        functools.partial(_gdn_kernel, lc=lc, ng=ng, nbk=nbk, has_state=has_state),
        grid=(nseq // step, nchunk),
        in_specs=in_specs,
        out_specs=[pl.BlockSpec((step, 1, lc, DN_WIDTH), lambda b, c: (b, c, 0, 0)), state_spec],
        out_shape=[jax.ShapeDtypeStruct((nseq, nchunk, lc, DN_WIDTH), F32),
                   jax.ShapeDtypeStruct((nseq, DN_HEADS, DN_DK, DN_DK), F32)],
        scratch_shapes=[pltpu.VMEM((step, TAIL + lc, DN_QKV), F32)],
        compiler_params=_cparams(("arbitrary", "arbitrary")),
        name="gdn_mixer_state" if has_state else "gdn_mixer",
    )(*args)


SSM_PAIRS = SSM_HEADS // 2
PAIR_W = 2 * SSM_HEADDIM
GROUP_W = SSM_WIDTH // 2


def _ssd_kernel(*refs, lc, has_state):
    if has_state:
        (xbc_ref, gate_ref, z_ref, buf_ref, s0_ref, cw_ref, cb_ref, gp_ref, dd_ref, nw_ref,
         reps_ref, repw_ref, o_ref, s_ref, xp_ref) = refs
    else:
        (xbc_ref, gate_ref, z_ref, cw_ref, cb_ref, gp_ref, dd_ref, nw_ref,
         reps_ref, repw_ref, o_ref, s_ref, xp_ref) = refs

    @pl.when(pl.program_id(1) == 0)
    def _():
        if has_state:
            xp_ref[0:TAIL, :] = buf_ref[0]
            s_ref[0] = s0_ref[0]
        else:
            xp_ref[0:TAIL, :] = jnp.zeros((TAIL, SSM_XBC), F32)
            s_ref[0] = jnp.zeros((SSM_PAIRS, PAIR_W, SSM_STATE), F32)

    xbc = _silu(_causal_conv(xbc_ref[0], xp_ref, cw_ref[...], lc) + cb_ref[...])
    xs = xbc[:, :SSM_WIDTH]

    dt = _softplus(gate_ref[0] + gp_ref[1:2, :])
    a = dt * (-jnp.exp(gp_ref[0:1, :]))
    row = lax.broadcasted_iota(jnp.int32, (lc, lc), 0)
    col = lax.broadcasted_iota(jnp.int32, (lc, lc), 1)
    incl = row >= col
    acs = _dot_sel_lhs(incl.astype(BF16), a)
    acs_t = jnp.transpose(acs)
    dt_b = _dot_sel_rhs(dt, reps_ref[...])
    acs_b = _dot_sel_rhs(acs, reps_ref[...])
    acs_w = _dot_sel_rhs(acs, repw_ref[...])
    eacs_b = jnp.exp(acs_b)
    acs_last_b = acs_b[lc - 1:lc, :]
    ends_b = jnp.exp(acs_last_b - acs_b)
    ecd_w = jnp.exp(acs_w[lc - 1:lc, :])
    xd = xs * dt_b
    first_head = lax.broadcasted_iota(jnp.int32, (lc, PAIR_W), 1) < SSM_HEADDIM

    ys = []
    ssq = [None, None]
    for p in range(SSM_PAIRS):
        g = p // (SSM_PAIRS // 2)
        psl = slice(p * PAIR_W, (p + 1) * PAIR_W)
        bm = xbc[:, SSM_WIDTH + g * SSM_STATE:SSM_WIDTH + (g + 1) * SSM_STATE]
        cm = xbc[:, SSM_WIDTH + 2 * SSM_STATE + g * SSM_STATE:SSM_WIDTH + 2 * SSM_STATE + (g + 1) * SSM_STATE]
        cb = _bdot_nt(cm, bm)
        xd_p = xd[:, psl]
        y_heads = []
        for hh in (2 * p, 2 * p + 1):
            lmat = _decay_matrix(acs_w[:, hh * LANES:hh * LANES + lc],
                                 acs_t[2 * DN_HEADS + hh:2 * DN_HEADS + hh + 1, :], incl)
            y_heads.append(_bdot(cb * lmat, xd_p))
        y = jnp.where(first_head, y_heads[0], y_heads[1])
        prev = s_ref[0, p]
        y = y + _bdot_nt(cm, prev) * eacs_b[:, psl]
        st = _bdot_tn(xd_p * ends_b[:, psl], bm)
        cd = jnp.concatenate(
            [jnp.broadcast_to(ecd_w[:, (2 * p) * LANES:(2 * p + 1) * LANES], (SSM_HEADDIM, SSM_STATE)),
             jnp.broadcast_to(ecd_w[:, (2 * p + 1) * LANES:(2 * p + 2) * LANES], (SSM_HEADDIM, SSM_STATE))],
            axis=0)
        s_ref[0, p] = prev * cd + st
        y = y + dd_ref[:, psl] * xs[:, psl]
        y = y * _silu(z_ref[0, :, psl])
        sq = jnp.sum(y * y, axis=-1, keepdims=True)
        ssq[g] = sq if ssq[g] is None else ssq[g] + sq
        ys.append(y)
    for p in range(SSM_PAIRS):
        g = p // (SSM_PAIRS // 2)
        psl = slice(p * PAIR_W, (p + 1) * PAIR_W)
        o_ref[0, :, psl] = ys[p] * lax.rsqrt(ssq[g] * (1.0 / GROUP_W) + EPS) * nw_ref[:, psl]


def ssd_mixer(proj3, row0, nseq, nchunk, lc, params, buf=None, s0=None):
    has_state = buf is not None

    def rb(b, c):
        return row0 + b * nchunk + c

    in_specs = [pl.BlockSpec((1, lc, SSM_XBC), lambda b, c: (rb(b, c), 0, COL_XBC // SSM_XBC)),
                pl.BlockSpec((1, lc, GATE_W), lambda b, c: (rb(b, c), 0, COL_GATE // GATE_W)),
                pl.BlockSpec((1, lc, SSM_WIDTH), lambda b, c: (rb(b, c), 0, COL_ZSSM // SSM_WIDTH))]
    args = [proj3, proj3, proj3]
    if has_state:
        in_specs += [pl.BlockSpec((1, TAIL, SSM_XBC), lambda b, c: (b, 0, 0)),
                     pl.BlockSpec((1, SSM_PAIRS, PAIR_W, SSM_STATE), lambda b, c: (b, 0, 0, 0))]
        args += [buf, s0]
    for prm in params:
        in_specs.append(pl.BlockSpec(prm.shape, lambda b, c: (0, 0)))
        args.append(prm)
    return pl.pallas_call(
        functools.partial(_ssd_kernel, lc=lc, has_state=has_state),
        grid=(nseq, nchunk),
        in_specs=in_specs,
        out_specs=[pl.BlockSpec((1, lc, SSM_WIDTH), lambda b, c: (b * nchunk + c, 0, 0)),
                   pl.BlockSpec((1, SSM_PAIRS, PAIR_W, SSM_STATE), lambda b, c: (b, 0, 0, 0))],
        out_shape=[jax.ShapeDtypeStruct((nseq * nchunk, lc, SSM_WIDTH), F32),
                   jax.ShapeDtypeStruct((nseq, SSM_PAIRS, PAIR_W, SSM_STATE), F32)],
        scratch_shapes=[pltpu.VMEM((TAIL + lc, SSM_XBC), F32)],
        compiler_params=_cparams(("arbitrary", "arbitrary")),
        name="ssd_mixer_state" if has_state else "ssd_mixer",
    )(*args)


def _xattn_kernel(q_ref, k_ref, v_ref, o_ref):
    for h in range(MEM_HEADS):
        sl = slice(h * MEM_HD, (h + 1) * MEM_HD)
        k = k_ref[0, :, sl]
        v = v_ref[0, :, sl]
        s = _bdot_nt(q_ref[0, :, sl], k) * (MEM_HD ** -0.5)
        s = s - jnp.max(s, axis=-1, keepdims=True)
        p = jnp.exp(s)
        p = p / jnp.sum(p, axis=-1, keepdims=True)
        o_ref[0, :, sl] = _bdot(p, v)


def xattn_core(q3, row0, nseq, ntile, tl, mem_k, mem_v, kv_col=(0, 0)):
    kv_specs = [pl.BlockSpec((1, MEM_LEN, D_MODEL), lambda b, t, c_=c_: (b, 0, c_)) for c_ in kv_col]
    return pl.pallas_call(
        _xattn_kernel,
        grid=(nseq, ntile),
        in_specs=[pl.BlockSpec((1, tl, D_MODEL), lambda b, t: (row0 + b * ntile + t, 0, 0))] + kv_specs,
        out_specs=pl.BlockSpec((1, tl, D_MODEL), lambda b, t: (b * ntile + t, 0, 0)),
        out_shape=jax.ShapeDtypeStruct((nseq * ntile, tl, D_MODEL), F32),
        compiler_params=_cparams(("arbitrary", "arbitrary")),
        name="xattn_core",
    )(q3, mem_k, mem_v)


def _top16_rows(s):
    n = s.shape[0]
    iota = lax.broadcasted_iota(jnp.int32, s.shape, 0).astype(F32)
    rank = jnp.full(s.shape, NOT_TOP, F32)
    vals = []
    v = s
    for a in range(PEER_TOPK):
        m = jnp.max(v, axis=0, keepdims=True)
        idx = jnp.min(jnp.where(v == m, iota, float(n)), axis=0, keepdims=True)
        hit = iota == idx
        rank = jnp.where(hit, float(a), rank)
        v = jnp.where(hit, -jnp.inf, v)
        vals.append(m)
    return jnp.concatenate(vals, axis=0), rank


def _pair_top16(s0v, s1v):
    t = s0v.shape[1]
    iota = lax.broadcasted_iota(jnp.int32, (PEER_TOPK, t), 0).astype(F32)
    n = jnp.zeros((PEER_TOPK, t), F32)
    front = s0v + s1v[0:1, :]
    top = s0v[0:1, :] + s1v[0:1, :]
    z = jnp.zeros((1, t), F32)
    for _ in range(PEER_TOPK):
        m = jnp.max(front, axis=0, keepdims=True)
        a_star = jnp.min(jnp.where(front == m, iota, float(PEER_TOPK)), axis=0, keepdims=True)
        hit = iota == a_star
        z = z + jnp.exp(m - top)
        n = jnp.where(hit, n + 1.0, n)
        nxt = jnp.full((PEER_TOPK, t), -jnp.inf, F32)
        for b in range(1, PEER_TOPK):
            nxt = jnp.where(n == float(b), s1v[b:b + 1, :], nxt)
        front = jnp.where(hit, s0v + nxt, front)
    return n, z


def _peer_route_kernel(x_ref, g_ref, wq_ref, sk_ref, xn_ref, e0_ref, ni_ref, e1_ref, rj_ref, q_ref):
    xn = _rmsnorm_rows(x_ref[...], g_ref[...]).astype(BF16)
    xn_ref[...] = xn
    q_ref[...] = lax.dot_general(wq_ref[...], xn, (((1,), (1,)), ((), ())),
                                 preferred_element_type=F32).astype(BF16)
    tt = x_ref.shape[0]
    for lb in range(tt // LANES):
        ls = slice(lb * LANES, (lb + 1) * LANES)
        for h in range(PEER_HEADS):
            halves = []
            for c in range(2):
                r0 = (2 * h + c) * PEER_HALF
                s = jnp.dot(sk_ref[c], q_ref[r0:r0 + PEER_HALF, ls],
                            preferred_element_type=F32)
                vals, rank = _top16_rows(s)
                halves.append((s, vals, rank))
            (s0, s0v, rank0), (s1, s1v, rank1) = halves
            n, z = _pair_top16(s0v, s1v)
            in0 = rank0 < float(PEER_TOPK)
            in1 = rank1 < float(PEER_TOPK)
            e0 = jnp.where(in0, jnp.exp(jnp.where(in0, s0 - s0v[0:1, :], 0.0)), 0.0) / z
            e1 = jnp.where(in1, jnp.exp(jnp.where(in1, s1 - s1v[0:1, :], 0.0)), 0.0)
            ni = jnp.zeros_like(rank0)
            for a in range(PEER_TOPK):
                ni = jnp.where(rank0 == float(a), n[a:a + 1, :], ni)
            e0_ref[h, :, ls] = e0
            ni_ref[h, :, ls] = ni
            e1_ref[h, :, ls] = e1.astype(BF16)
            rj_ref[h, :, ls] = rank1.astype(BF16)


def peer_route(x, g, wq_t, sub_keys, *, tt):
    n = x.shape[0]
    fspec = pl.BlockSpec((PEER_HEADS, PEER_NKEYS, tt), lambda i: (0, 0, i))

    def fac(dtype):
        return jax.ShapeDtypeStruct((PEER_HEADS, PEER_NKEYS, n), dtype)

    return pl.pallas_call(
        _peer_route_kernel,
        grid=(n // tt,),
        in_specs=[pl.BlockSpec((tt, D_MODEL), lambda i: (i, 0)),
                  pl.BlockSpec((1, D_MODEL), lambda i: (0, 0)),
                  pl.BlockSpec(wq_t.shape, lambda i: (0, 0)),
                  pl.BlockSpec(sub_keys.shape, lambda i: (0, 0, 0))],
        out_specs=[pl.BlockSpec((tt, D_MODEL), lambda i: (i, 0)), fspec, fspec, fspec, fspec],
        out_shape=[jax.ShapeDtypeStruct((n, D_MODEL), BF16), fac(F32), fac(F32), fac(BF16), fac(BF16)],
        scratch_shapes=[pltpu.VMEM((PEER_HEADS * 2 * PEER_HALF, tt), BF16)],
        compiler_params=_cparams(("arbitrary",)),
        name="peer_route",
    )(x, g.reshape(1, D_MODEL), wq_t, sub_keys)


def _peer_dense_kernel(xn_ref, u_ref, vt_ref, e0_ref, ni_ref, e1_ref, rj_ref, x_ref, gf_ref,
                       yp_ref, ys_ref, acc_ref, a_ref, p_ref, *, te, np_tiles):
    i_tok = pl.program_id(0)
    e = pl.program_id(1)

    @pl.when(e == 0)
    def _():
        acc_ref[...] = jnp.zeros_like(acc_ref)

    a_ref[...] = lax.dot_general(u_ref[...], xn_ref[...], (((1,), (1,)), ((), ())),
                                 preferred_element_type=F32)
    zero = jnp.zeros((), BF16)
    for ii in range(te // PEER_NKEYS):
        i = e * (te // PEER_NKEYS) + ii
        w = None
        for h in range(PEER_HEADS):
            e0r = e0_ref[h, pl.ds(i, 1), :].astype(BF16)
            nir = ni_ref[h, pl.ds(i, 1), :].astype(BF16)
            wh = jnp.where(rj_ref[h] < nir, e0r * e1_ref[h], zero)
            w = wh if w is None else w + wh
        a = a_ref[ii * PEER_NKEYS:(ii + 1) * PEER_NKEYS, :]
        act = 0.5 * a * (1.0 + lax.erf(a * (2.0 ** -0.5)))
        p_ref[ii * PEER_NKEYS:(ii + 1) * PEER_NKEYS, :] = w * act.astype(BF16)
    acc_ref[...] += jnp.dot(vt_ref[...], p_ref[...], preferred_element_type=F32)

    @pl.when(e == pl.num_programs(1) - 1)
    def _():
        y = _rmsnorm_rows(x_ref[...] + jnp.transpose(acc_ref[...]), gf_ref[...])

        @pl.when(i_tok < np_tiles)
        def _():
            yp_ref[...] = y

        @pl.when(i_tok >= np_tiles)
        def _():
            ys_ref[...] = y


def peer_dense(xn, u, v_t, e0, ni, e1, rj, x, g_final, n_p, *, tt, te):
    n = xn.shape[0]
    np_tiles = n_p // tt
    fspec = pl.BlockSpec((PEER_HEADS, PEER_NKEYS, tt), lambda i, e: (0, 0, i))
    return pl.pallas_call(
        functools.partial(_peer_dense_kernel, te=te, np_tiles=np_tiles),
        grid=(n // tt, PEER_EXPERTS // te),
        in_specs=[pl.BlockSpec((tt, D_MODEL), lambda i, e: (i, 0)),
                  pl.BlockSpec((te, D_MODEL), lambda i, e: (e, 0)),
                  pl.BlockSpec((D_MODEL, te), lambda i, e: (0, e)),
                  fspec, fspec, fspec, fspec,
                  pl.BlockSpec((tt, D_MODEL), lambda i, e: (i, 0)),
                  pl.BlockSpec((1, D_MODEL), lambda i, e: (0, 0))],
        out_specs=[pl.BlockSpec((tt, D_MODEL), lambda i, e: (jnp.minimum(i, np_tiles - 1), 0)),
                   pl.BlockSpec((tt, D_MODEL), lambda i, e: (jnp.maximum(i - np_tiles, 0), 0))],
        out_shape=[jax.ShapeDtypeStruct((n_p, D_MODEL), F32),
                   jax.ShapeDtypeStruct((n - n_p, D_MODEL), F32)],
        scratch_shapes=[pltpu.VMEM((D_MODEL, tt), F32),
                        pltpu.VMEM((te, tt), F32),
                        pltpu.VMEM((te, tt), BF16)],
        compiler_params=_cparams(("arbitrary", "arbitrary")),
        name="peer_dense",
    )(xn, u, v_t, e0, ni, e1, rj, x, g_final.reshape(1, D_MODEL))


def _lane_row(vec, start, width=GATE_W):
    return jnp.zeros((1, width), F32).at[0, start:start + vec.shape[0]].set(vec.astype(F32))


def _rep_matrix(first_row, heads, lanes_per_head, rows=GATE_W):
    r = jnp.arange(rows)[:, None]
    c = jnp.arange(heads * lanes_per_head)[None, :]
    return ((c // lanes_per_head) + first_row == r).astype(BF16)


def _pad_tail(buf):
    return jnp.pad(buf, ((0, 0), (TAIL - (CONV_K - 1), 0), (0, 0)))


def kernel(x_prompt, x_sample, state_dn_conv, state_dn, state_ssm_conv, state_ssm, cache_mem_k, cache_mem_v,
           mem_prompt, g_mix, w_in, dn_conv_w, dn_A_log, dn_dt_bias, dn_norm_w, ssm_conv_w, ssm_conv_b,
           ssm_A_log, ssm_dt_bias, ssm_D, ssm_norm_w, w_out, g_xattn, g_mem, w_xq, w_mkv, w_xo, g_ffn, w_pq,
           peer_sub_keys, peer_u, peer_v, g_final):
    depth = g_mix.shape[0]
    assert depth == 1
    bp, lp, d = x_prompt.shape
    bs, ls, _ = x_sample.shape
    n_p, n_s = bp * lp, bs * ls
    n = n_p + n_s
    ncp = lp // CHUNK
    tm = 512
    x = (x_prompt.reshape(n_p, d), x_sample.reshape(n_s, d))

    wi = w_in[0]
    o_xbc = DN_QKV
    o_zdn = o_xbc + SSM_XBC
    o_zssm = o_zdn + DN_WIDTH
    o_small = o_zssm + SSM_WIDTH
    n_small = 2 * DN_HEADS + SSM_HEADS
    w_cat = jnp.concatenate([
        wi[:, :DN_QKV], wi[:, o_xbc:o_xbc + SSM_XBC],
        wi[:, o_small:o_small + n_small], jnp.zeros((d, COL_ZDN - COL_GATE - n_small), F32),
        wi[:, o_zdn:o_zdn + DN_WIDTH], wi[:, o_zssm:o_zssm + SSM_WIDTH]], axis=1).astype(BF16)

    proj = norm_matmul(x, g_mix[0], w_cat, tm=tm, tn=1024)
    proj_c = proj.reshape(n // CHUNK, CHUNK, PROJ_COLS)
    proj_s = proj.reshape(n // ls, ls, PROJ_COLS)

    gdn_params = (dn_conv_w[0],
                  jnp.concatenate([_lane_row(dn_A_log[0], DN_HEADS), _lane_row(dn_dt_bias[0], DN_HEADS)], axis=0),
                  dn_norm_w[0].reshape(1, DN_DK),
                  _rep_matrix(0, DN_HEADS, DN_DK), _rep_matrix(DN_HEADS, DN_HEADS, DN_DK))
    ssd_params = (ssm_conv_w[0], ssm_conv_b[0].reshape(1, SSM_XBC),
                  jnp.concatenate([_lane_row(ssm_A_log[0], 2 * DN_HEADS), _lane_row(ssm_dt_bias[0], 2 * DN_HEADS)],
                                  axis=0),
                  jnp.repeat(ssm_D[0], SSM_HEADDIM).reshape(1, SSM_WIDTH),
                  ssm_norm_w[0].reshape(1, SSM_WIDTH),
                  _rep_matrix(2 * DN_HEADS, SSM_HEADS, SSM_HEADDIM), _rep_matrix(2 * DN_HEADS, SSM_HEADS, LANES))

    odn_p, p_dn = gdn_mixer(proj_c, 0, bp, ncp, CHUNK, 2, 1, gdn_params)
    odn_s, s_dn = gdn_mixer(proj_s, n_p // ls, bs, 1, ls, 1, 4, gdn_params,
                            buf=_pad_tail(state_dn_conv[0]), s0=state_dn[0])
    ossm_p, p_ss = ssd_mixer(proj_c, 0, bp, ncp, CHUNK, ssd_params)
    ossm_s, s_ss = ssd_mixer(proj_s, n_p // ls, bs, 1, ls, ssd_params,
                             buf=_pad_tail(state_ssm_conv[0]),
                             s0=state_ssm[0].reshape(bs, SSM_PAIRS, PAIR_W, SSM_STATE))
    wo = w_out[0].astype(BF16)
    x1 = mixer_out((odn_p.reshape(n_p, DN_WIDTH), odn_s.reshape(n_s, DN_WIDTH)),
                   (ossm_p.reshape(n_p, SSM_WIDTH), ossm_s.reshape(n_s, SSM_WIDTH)),
                   wo[:DN_WIDTH], wo[DN_WIDTH:], x, tm=tm)

    last_c = slice(ncp - 1, bp * ncp, ncp)
    hist_c = slice(CHUNK - (CONV_K - 1), CHUNK)
    hist_s = slice(ls - (CONV_K - 1), ls)
    p_dnc = proj_c[last_c, hist_c, COL_QKV:COL_QKV + DN_QKV]
    p_sc = proj_c[last_c, hist_c, COL_XBC:COL_XBC + SSM_XBC]
    s_dnc = proj_s[n_p // ls:, hist_s, COL_QKV:COL_QKV + DN_QKV]
    s_sc = proj_s[n_p // ls:, hist_s, COL_XBC:COL_XBC + SSM_XBC]

    mkv = norm_matmul(mem_prompt.reshape(bp * MEM_LEN, d), g_mem[0], w_mkv[0].astype(BF16), tm=tm, tn=1024)
    mkv3 = mkv.reshape(bp, MEM_LEN, 2 * d)
    q = norm_matmul(x1, g_xattn[0], w_xq[0].astype(BF16), tm=tm, tn=1024)
    tl = 512
    a_p = xattn_core(q.reshape(n // tl, tl, d), 0, bp, lp // tl, tl, mkv3, mkv3, kv_col=(0, 1))
    a_s = xattn_core(q.reshape(n // ls, ls, d), n_p // ls, bs, 1, ls,
                     cache_mem_k[0].reshape(bs, MEM_LEN, d), cache_mem_v[0].reshape(bs, MEM_LEN, d))
    x2 = matmul_res((a_p.reshape(n_p, d), a_s.reshape(n_s, d)), w_xo[0].astype(BF16), x1, tm=tm)

    xn, e0, ni, e1, rj = peer_route(x2, g_ffn[0], jnp.transpose(w_pq[0]).astype(BF16),
                                    peer_sub_keys[0].astype(BF16), tt=256)
    y_p, y_s = peer_dense(xn, peer_u[0].astype(BF16), jnp.transpose(peer_v[0]).astype(BF16),
                          e0, ni, e1, rj, x2, g_final, n_p, tt=512, te=1024)

    def heads(m):
        return m.reshape(bp, MEM_LEN, MEM_HEADS, MEM_HD)[None]

    return (y_p.reshape(bp, lp, d), y_s.reshape(bs, ls, d),
            p_dnc[None], p_dn[None], p_sc[None], p_ss.reshape(bp, SSM_HEADS, SSM_HEADDIM, SSM_STATE)[None],
            heads(mkv3[:, :, :d]), heads(mkv3[:, :, d:]),
            s_dnc[None], s_dn[None], s_sc[None], s_ss.reshape(bs, SSM_HEADS, SSM_HEADDIM, SSM_STATE)[None])
```

```python
import functools

import jax
import jax.numpy as jnp
from jax import lax
from jax.experimental import pallas as pl
from jax.experimental.pallas import tpu as pltpu

F32 = jnp.float32
BF16 = jnp.bfloat16
EPS = 1e-6

D_MODEL = 1024
CONV_K = 4
CHUNK = 64
DN_HEADS = 8
DN_DK = 128
DN_WIDTH = 1024
DN_QKV = 3072
SSM_HEADS = 16
SSM_HEADDIM = 64
SSM_STATE = 128
SSM_WIDTH = 1024
SSM_XBC = 1536
MEM_LEN = 256
MEM_HEADS = 4
MEM_HD = 256
PEER_HEADS = 8
PEER_NKEYS = 128
PEER_EXPERTS = PEER_NKEYS * PEER_NKEYS
PEER_TOPK = 16
PEER_HALF = 128

COL_QKV = 0
COL_XBC = 3072
COL_GATE = 4608
COL_ZDN = 5120
COL_ZSSM = 6144
PROJ_COLS = 7168
GATE_W = 128
TAIL = 8
INV_BLOCK = 16
NOT_TOP = 99.0
LANES = 128
PEER_SUB = 1024

VMEM_LIMIT = 56 * 1024 * 1024


def _cparams(sem):
    return pltpu.CompilerParams(dimension_semantics=sem, vmem_limit_bytes=VMEM_LIMIT)


def _bdot(a, b):
    return jnp.dot(a.astype(BF16), b.astype(BF16), preferred_element_type=F32)


def _bdot_nt(a, b):
    return lax.dot_general(a.astype(BF16), b.astype(BF16), (((1,), (1,)), ((), ())),
                           preferred_element_type=F32)


def _bdot_tn(a, b):
    return lax.dot_general(a.astype(BF16), b.astype(BF16), (((0,), (0,)), ((), ())),
                           preferred_element_type=F32)


def _split3(x):
    hi = x.astype(BF16)
    r = x - hi.astype(F32)
    mid = r.astype(BF16)
    lo = (r - mid.astype(F32)).astype(BF16)
    return hi, mid, lo


def _dot_sel_rhs(x, sel):
    hi, mid, lo = _split3(x)
    d = functools.partial(jnp.dot, preferred_element_type=F32)
    return d(hi, sel) + d(mid, sel) + d(lo, sel)


def _dot_sel_lhs(sel, x):
    hi, mid, lo = _split3(x)
    d = functools.partial(jnp.dot, preferred_element_type=F32)
    return d(sel, hi) + d(sel, mid) + d(sel, lo)


def _silu(x):
    return x * jax.nn.sigmoid(x)


def _softplus(x):
    return jnp.maximum(x, 0.0) + jnp.log1p(jnp.exp(-jnp.abs(x)))


def _unit_lower_inverse(lms, n):
    row = lax.broadcasted_iota(jnp.int32, (n, n), 0)
    col = lax.broadcasted_iota(jnp.int32, (n, n), 1)
    eye = (row == col).astype(F32)

    def nilpotent_inverse(xs, index):
        invs = [eye - x for x in xs]
        ps = xs
        k = 2
        while k < index:
            ps = [_bdot(p, p) for p in ps]
            invs = [inv + _bdot(inv, p) for inv, p in zip(invs, ps)]
            k *= 2
        return invs

    if n <= INV_BLOCK:
        return nilpotent_inverse(lms, n)
    shift = INV_BLOCK.bit_length() - 1
    same = jnp.right_shift(row, shift) == jnp.right_shift(col, shift)
    dinvs = nilpotent_inverse([jnp.where(same, lm, 0.0) for lm in lms], INV_BLOCK)
    fs = [_bdot(dinv, jnp.where(same, 0.0, lm)) for dinv, lm in zip(dinvs, lms)]
    finvs = nilpotent_inverse(fs, n // INV_BLOCK)
    return [_bdot(finv, dinv) for finv, dinv in zip(finvs, dinvs)]


def _causal_conv(x, xp_ref, w, lc):
    xp_ref[TAIL:TAIL + lc, :] = x
    y = x * w[3:4, :]
    for s in range(1, CONV_K):
        y = y + xp_ref[TAIL - s:TAIL - s + lc, :] * w[CONV_K - 1 - s:CONV_K - s, :]
    xp_ref[0:TAIL, :] = xp_ref[lc:lc + TAIL, :]
    return y


def _decay_matrix(col, row, lower_incl):
    diff = jnp.where(lower_incl, col - row, 0.0)
    return jnp.where(lower_incl, jnp.exp(diff), 0.0)


def _pair_specs(tm, k, np_tiles, two_axes):
    if two_axes:
        return [pl.BlockSpec((tm, k), lambda i, j: (jnp.minimum(i, np_tiles - 1), 0)),
                pl.BlockSpec((tm, k), lambda i, j: (jnp.maximum(i - np_tiles, 0), 0))]
    return [pl.BlockSpec((tm, k), lambda i: (jnp.minimum(i, np_tiles - 1), 0)),
            pl.BlockSpec((tm, k), lambda i: (jnp.maximum(i - np_tiles, 0), 0))]


def _pick(is_prompt, p_ref, s_ref):
    return jnp.where(is_prompt, p_ref[...], s_ref[...])


def _rmsnorm_rows(x, g):
    ms = jnp.mean(x * x, axis=-1, keepdims=True)
    return x * lax.rsqrt(ms + EPS) * g


def _norm_matmul_kernel(*refs, np_tiles):
    if np_tiles is None:
        x_ref, g_ref, w_ref, o_ref, xn_ref = refs
    else:
        xp_ref, xs_ref, g_ref, w_ref, o_ref, xn_ref = refs
        is_prompt = pl.program_id(0) < np_tiles

    @pl.when(pl.program_id(1) == 0)
    def _():
        x = x_ref[...] if np_tiles is None else _pick(is_prompt, xp_ref, xs_ref)
        xn_ref[...] = _rmsnorm_rows(x, g_ref[...]).astype(BF16)

    o_ref[...] = jnp.dot(xn_ref[...], w_ref[...], preferred_element_type=F32).astype(o_ref.dtype)


def norm_matmul(x, g, w, *, tm, tn, out_dtype=F32):
    pair = isinstance(x, tuple)
    n = sum(a.shape[0] for a in x) if pair else x.shape[0]
    k, m = w.shape
    assert n % tm == 0 and m % tn == 0
    if pair:
        assert x[0].shape[0] % tm == 0
        np_tiles = x[0].shape[0] // tm
        x_specs = _pair_specs(tm, k, np_tiles, True)
        x_args = list(x)
    else:
        np_tiles = None
        x_specs = [pl.BlockSpec((tm, k), lambda i, j: (i, 0))]
        x_args = [x]
    return pl.pallas_call(
        functools.partial(_norm_matmul_kernel, np_tiles=np_tiles),
        grid=(n // tm, m // tn),
        in_specs=x_specs + [pl.BlockSpec((1, k), lambda i, j: (0, 0)),
                            pl.BlockSpec((k, tn), lambda i, j: (0, j))],
        out_specs=pl.BlockSpec((tm, tn), lambda i, j: (i, j)),
        out_shape=jax.ShapeDtypeStruct((n, m), out_dtype),
        scratch_shapes=[pltpu.VMEM((tm, k), BF16)],
        compiler_params=_cparams(("arbitrary", "arbitrary")),
        name="norm_matmul",
    )(*x_args, g.reshape(1, k), w)


def _mixer_out_kernel(dp_ref, ds_ref, sp_ref, ss_ref, w1_ref, w2_ref, xp_ref, xs_ref, o_ref, *, np_tiles):
    is_prompt = pl.program_id(0) < np_tiles
    a1 = _pick(is_prompt, dp_ref, ds_ref).astype(BF16)
    a2 = _pick(is_prompt, sp_ref, ss_ref).astype(BF16)
    acc = jnp.dot(a1, w1_ref[...], preferred_element_type=F32)
    acc = acc + jnp.dot(a2, w2_ref[...], preferred_element_type=F32)
    o_ref[...] = _pick(is_prompt, xp_ref, xs_ref) + acc


def mixer_out(o_dn, o_ssm, w1, w2, x, *, tm):
    n = x[0].shape[0] + x[1].shape[0]
    k = w1.shape[0]
    m = w1.shape[1]
    np_tiles = x[0].shape[0] // tm
    wspec = pl.BlockSpec((k, m), lambda i: (0, 0))
    return pl.pallas_call(
        functools.partial(_mixer_out_kernel, np_tiles=np_tiles),
        grid=(n // tm,),
        in_specs=(_pair_specs(tm, k, np_tiles, False) + _pair_specs(tm, k, np_tiles, False)
                  + [wspec, wspec] + _pair_specs(tm, m, np_tiles, False)),
        out_specs=pl.BlockSpec((tm, m), lambda i: (i, 0)),
        out_shape=jax.ShapeDtypeStruct((n, m), F32),
        compiler_params=_cparams(("arbitrary",)),
        name="mixer_out",
    )(*o_dn, *o_ssm, w1, w2, *x)


def _matmul_res_kernel(ap_ref, as_ref, w_ref, r_ref, o_ref, *, np_tiles):
    a = _pick(pl.program_id(0) < np_tiles, ap_ref, as_ref).astype(BF16)
    o_ref[...] = r_ref[...] + jnp.dot(a, w_ref[...], preferred_element_type=F32)


def matmul_res(a, w, res, *, tm):
    n, m = res.shape
    k = w.shape[0]
    np_tiles = a[0].shape[0] // tm
    return pl.pallas_call(
        functools.partial(_matmul_res_kernel, np_tiles=np_tiles),
        grid=(n // tm,),
        in_specs=_pair_specs(tm, k, np_tiles, False) + [pl.BlockSpec((k, m), lambda i: (0, 0)),
                                                        pl.BlockSpec((tm, m), lambda i: (i, 0))],
        out_specs=pl.BlockSpec((tm, m), lambda i: (i, 0)),
        out_shape=jax.ShapeDtypeStruct((n, m), F32),
        compiler_params=_cparams(("arbitrary",)),
        name="matmul_res",
    )(*a, w, res)


def _gdn_kernel(*refs, lc, ng, nbk, has_state):
    refs = list(refs)

    def take(k):
        out = refs[:k]
        del refs[:k]
        return out

    qkv_refs, gate_refs, z_refs = take(ng), take(ng), take(ng)
    if has_state:
        buf_ref, s0_ref = take(2)
    cw_ref, gp_ref, nw_ref, repb_ref, repg_ref, o_ref, s_ref, xp_ref = take(8)
    nseq = ng * nbk
    where = [divmod(j, nbk) for j in range(nseq)]

    @pl.when(pl.program_id(1) == 0)
    def _():
        if has_state:
            xp_ref[:, 0:TAIL, :] = buf_ref[...]
            s_ref[...] = s0_ref[...]
        else:
            xp_ref[:, 0:TAIL, :] = jnp.zeros((nseq, TAIL, DN_QKV), F32)
            s_ref[...] = jnp.zeros(s_ref.shape, F32)

    row = lax.broadcasted_iota(jnp.int32, (lc, lc), 0)
    col = lax.broadcasted_iota(jnp.int32, (lc, lc), 1)
    incl = row >= col
    strict = row > col
    tri = incl.astype(BF16)

    seqs = []
    for j, (g_, k_) in enumerate(where):
        qkv = _silu(_causal_conv(qkv_refs[g_][k_], xp_ref.at[j], cw_ref[...], lc))
        gate = gate_refs[g_][k_]
        beta_all = jax.nn.sigmoid(gate)
        g_all = -jnp.exp(gp_ref[0:1, :]) * _softplus(gate + gp_ref[1:2, :])
        gc = _dot_sel_lhs(tri, g_all)
        gc_b = _dot_sel_rhs(gc, repg_ref[...])
        gc_last_b = gc_b[lc - 1:lc, :]
        seqs.append(dict(qkv=qkv, gc_t=jnp.transpose(gc), gc_b=gc_b,
                         beta_b=_dot_sel_rhs(beta_all, repb_ref[...]),
                         egc_b=jnp.exp(gc_b), ekd_b=jnp.exp(gc_last_b - gc_b), egt_b=jnp.exp(gc_last_b)))

    units = [(j, h) for j in range(nseq) for h in range(DN_HEADS)]
    qs, ks, vbs, kbegs, decays = [], [], [], [], []
    for j, h in units:
        sq = seqs[j]
        sl = slice(h * DN_DK, (h + 1) * DN_DK)
        qh = sq["qkv"][:, h * DN_DK:(h + 1) * DN_DK]
        kh = sq["qkv"][:, DN_WIDTH + h * DN_DK:DN_WIDTH + (h + 1) * DN_DK]
        vh = sq["qkv"][:, 2 * DN_WIDTH + h * DN_DK:2 * DN_WIDTH + (h + 1) * DN_DK]
        qh = qh * lax.rsqrt(jnp.sum(qh * qh, axis=-1, keepdims=True) + EPS) * (DN_DK ** -0.5)
        kh = kh * lax.rsqrt(jnp.sum(kh * kh, axis=-1, keepdims=True) + EPS)
        bh = sq["beta_b"][:, sl]
        qs.append(qh)
        ks.append(kh)
        vbs.append(vh * bh)
        kbegs.append((kh * bh, sq["egc_b"][:, sl]))
        decays.append(_decay_matrix(sq["gc_b"][:, h * DN_DK:h * DN_DK + lc],
                                    sq["gc_t"][DN_HEADS + h:DN_HEADS + h + 1, :], incl))
    lowers = [jnp.where(strict, _bdot_nt(kb, kh) * dec, 0.0)
              for (kb, _), kh, dec in zip(kbegs, ks, decays)]
    attns = [_bdot_nt(qh, kh) * dec for qh, kh, dec in zip(qs, ks, decays)]
    tinvs = _unit_lower_inverse(lowers, lc)
    us = [_bdot(t, vb) for t, vb in zip(tinvs, vbs)]
    ws = [_bdot(t, kb * eg) for t, (kb, eg) in zip(tinvs, kbegs)]
    states = [s_ref[j, h] for j, h in units]
    v_news = [u - _bdot(w, s) for u, w, s in zip(us, ws, states)]
    os_ = [_bdot(qh * eg, s) + _bdot(a, vn)
           for qh, (_, eg), s, a, vn in zip(qs, kbegs, states, attns, v_news)]
    for (j, h), kh, s, vn, o in zip(units, ks, states, v_news, os_):
        g_, k_ = where[j]
        sl = slice(h * DN_DK, (h + 1) * DN_DK)
        sq = seqs[j]
        s_ref[j, h] = s * sq["egt_b"][:, sl] + _bdot_tn(kh * sq["ekd_b"][:, sl], vn)
        o = o * lax.rsqrt(jnp.mean(o * o, axis=-1, keepdims=True) + EPS) * nw_ref[...]
        o_ref[j, 0, :, sl] = o * _silu(z_refs[g_][k_, :, sl])


def gdn_mixer(proj3, row0, nseq, nchunk, lc, ng, nbk, params, buf=None, s0=None):
    has_state = buf is not None
    step = ng * nbk
    assert nseq % step == 0 and row0 % nbk == 0 and (nbk == 1 or nchunk == 1)

    def proj_specs(width, col_block):
        return [pl.BlockSpec((nbk, lc, width),
                             lambda b, c, g_=g_: (row0 // nbk + (b * ng + g_) * nchunk + c, 0, col_block))
                for g_ in range(ng)]

    in_specs = (proj_specs(DN_QKV, COL_QKV // DN_QKV) + proj_specs(GATE_W, COL_GATE // GATE_W)
                + proj_specs(DN_WIDTH, COL_ZDN // DN_WIDTH))
    args = [proj3] * (3 * ng)
    state_spec = pl.BlockSpec((step, DN_HEADS, DN_DK, DN_DK), lambda b, c: (b, 0, 0, 0))
    if has_state:
        in_specs += [pl.BlockSpec((step, TAIL, DN_QKV), lambda b, c: (b, 0, 0)), state_spec]
        args += [buf, s0]
    for prm in params:
        in_specs.append(pl.BlockSpec(prm.shape, lambda b, c: (0, 0)))
        args.append(prm)
    return pl.pallas_call(
        functools.partial(_gdn_kernel, lc=lc, ng=ng, nbk=nbk, has_state=has_state),
        grid=(nseq // step, nchunk),
        in_specs=in_specs,
        out_specs=[pl.BlockSpec((step, 1, lc, DN_WIDTH), lambda b, c: (b, c, 0, 0)), state_spec],
        out_shape=[jax.ShapeDtypeStruct((nseq, nchunk, lc, DN_WIDTH), F32),
                   jax.ShapeDtypeStruct((nseq, DN_HEADS, DN_DK, DN_DK), F32)],
        scratch_shapes=[pltpu.VMEM((step, TAIL + lc, DN_QKV), F32)],
        compiler_params=_cparams(("arbitrary", "arbitrary")),
        name="gdn_mixer_state" if has_state else "gdn_mixer",
    )(*args)


SSM_PAIRS = SSM_HEADS // 2
PAIR_W = 2 * SSM_HEADDIM
GROUP_W = SSM_WIDTH // 2


def _ssd_kernel(*refs, lc, ng, nbk, has_state):
    refs = list(refs)

    def take(k):
        out = refs[:k]
        del refs[:k]
        return out

    xbc_refs, gate_refs, z_refs = take(ng), take(ng), take(ng)
    if has_state:
        buf_ref, s0_ref = take(2)
    (cw_ref, cb_ref, gp_ref, dd_ref, nw_ref, reps_ref, repw_ref, o_ref, s_ref, xp_ref) = take(10)
    nseq = ng * nbk
    where = [divmod(j, nbk) for j in range(nseq)]

    @pl.when(pl.program_id(1) == 0)
    def _():
        if has_state:
            xp_ref[:, 0:TAIL, :] = buf_ref[...]
            s_ref[...] = s0_ref[...]
        else:
            xp_ref[:, 0:TAIL, :] = jnp.zeros((nseq, TAIL, SSM_XBC), F32)
            s_ref[...] = jnp.zeros(s_ref.shape, F32)

    row = lax.broadcasted_iota(jnp.int32, (lc, lc), 0)
    col = lax.broadcasted_iota(jnp.int32, (lc, lc), 1)
    incl = row >= col
    tri = incl.astype(BF16)
    first_head = lax.broadcasted_iota(jnp.int32, (lc, PAIR_W), 1) < SSM_HEADDIM

    seqs = []
    for j, (g_, k_) in enumerate(where):
        xbc = _silu(_causal_conv(xbc_refs[g_][k_], xp_ref.at[j], cw_ref[...], lc) + cb_ref[...])
        xs = xbc[:, :SSM_WIDTH]
        dt = _softplus(gate_refs[g_][k_] + gp_ref[1:2, :])
        acs = _dot_sel_lhs(tri, dt * (-jnp.exp(gp_ref[0:1, :])))
        acs_b = _dot_sel_rhs(acs, reps_ref[...])
        acs_w = _dot_sel_rhs(acs, repw_ref[...])
        seqs.append(dict(xbc=xbc, xs=xs, acs_t=jnp.transpose(acs), acs_w=acs_w,
                         xd=xs * _dot_sel_rhs(dt, reps_ref[...]),
                         eacs_b=jnp.exp(acs_b), ends_b=jnp.exp(acs_b[lc - 1:lc, :] - acs_b),
                         ecd_w=jnp.exp(acs_w[lc - 1:lc, :])))

    def bmat(sq, g):
        return sq["xbc"][:, SSM_WIDTH + g * SSM_STATE:SSM_WIDTH + (g + 1) * SSM_STATE]

    def cmat(sq, g):
        o = SSM_WIDTH + 2 * SSM_STATE
        return sq["xbc"][:, o + g * SSM_STATE:o + (g + 1) * SSM_STATE]

    cbs = [[_bdot_nt(cmat(sq, g), bmat(sq, g)) for g in range(2)] for sq in seqs]
    units = [(j, p) for j in range(nseq) for p in range(SSM_PAIRS)]
    lmats = [[_decay_matrix(seqs[j]["acs_w"][:, hh * LANES:hh * LANES + lc],
                            seqs[j]["acs_t"][2 * DN_HEADS + hh:2 * DN_HEADS + hh + 1, :], incl)
              for hh in (2 * p, 2 * p + 1)] for j, p in units]
    prevs = [s_ref[j, p] for j, p in units]
    y_diag, y_off, sts = [], [], []
    for (j, p), lm, prev in zip(units, lmats, prevs):
        sq = seqs[j]
        g = p // (SSM_PAIRS // 2)
        psl = slice(p * PAIR_W, (p + 1) * PAIR_W)
        xd_p = sq["xd"][:, psl]
        y_diag.append([_bdot(cbs[j][g] * lm[0], xd_p), _bdot(cbs[j][g] * lm[1], xd_p)])
        y_off.append(_bdot_nt(cmat(sq, g), prev))
        sts.append(_bdot_tn(xd_p * sq["ends_b"][:, psl], bmat(sq, g)))
    ys = []
    ssq = [[None, None] for _ in range(nseq)]
    for (j, p), yd, yo, st, prev in zip(units, y_diag, y_off, sts, prevs):
        sq = seqs[j]
        g_, k_ = where[j]
        g = p // (SSM_PAIRS // 2)
        psl = slice(p * PAIR_W, (p + 1) * PAIR_W)
        cd = jnp.concatenate(
            [jnp.broadcast_to(sq["ecd_w"][:, (2 * p) * LANES:(2 * p + 1) * LANES], (SSM_HEADDIM, SSM_STATE)),
             jnp.broadcast_to(sq["ecd_w"][:, (2 * p + 1) * LANES:(2 * p + 2) * LANES], (SSM_HEADDIM, SSM_STATE))],
            axis=0)
        s_ref[j, p] = prev * cd + st
        y = jnp.where(first_head, yd[0], yd[1]) + yo * sq["eacs_b"][:, psl] + dd_ref[:, psl] * sq["xs"][:, psl]
        y = y * _silu(z_refs[g_][k_, :, psl])
        s2 = jnp.sum(y * y, axis=-1, keepdims=True)
        ssq[j][g] = s2 if ssq[j][g] is None else ssq[j][g] + s2
        ys.append(y)
    for (j, p), y in zip(units, ys):
        g = p // (SSM_PAIRS // 2)
        psl = slice(p * PAIR_W, (p + 1) * PAIR_W)
        o_ref[j, 0, :, psl] = y * lax.rsqrt(ssq[j][g] * (1.0 / GROUP_W) + EPS) * nw_ref[:, psl]


def ssd_mixer(proj3, row0, nseq, nchunk, lc, ng, nbk, params, buf=None, s0=None):
    has_state = buf is not None
    step = ng * nbk
    assert nseq % step == 0 and row0 % nbk == 0 and (nbk == 1 or nchunk == 1)

    def proj_specs(width, col_block):
        return [pl.BlockSpec((nbk, lc, width),
                             lambda b, c, g_=g_: (row0 // nbk + (b * ng + g_) * nchunk + c, 0, col_block))
                for g_ in range(ng)]

    in_specs = (proj_specs(SSM_XBC, COL_XBC // SSM_XBC) + proj_specs(GATE_W, COL_GATE // GATE_W)
                + proj_specs(SSM_WIDTH, COL_ZSSM // SSM_WIDTH))
    args = [proj3] * (3 * ng)
    state_spec = pl.BlockSpec((step, SSM_PAIRS, PAIR_W, SSM_STATE), lambda b, c: (b, 0, 0, 0))
    if has_state:
        in_specs += [pl.BlockSpec((step, TAIL, SSM_XBC), lambda b, c: (b, 0, 0)), state_spec]
        args += [buf, s0]
    for prm in params:
        in_specs.append(pl.BlockSpec(prm.shape, lambda b, c: (0, 0)))
        args.append(prm)
    return pl.pallas_call(
        functools.partial(_ssd_kernel, lc=lc, ng=ng, nbk=nbk, has_state=has_state),
        grid=(nseq // step, nchunk),
        in_specs=in_specs,
        out_specs=[pl.BlockSpec((step, 1, lc, SSM_WIDTH), lambda b, c: (b, c, 0, 0)), state_spec],
        out_shape=[jax.ShapeDtypeStruct((nseq, nchunk, lc, SSM_WIDTH), F32),
                   jax.ShapeDtypeStruct((nseq, SSM_PAIRS, PAIR_W, SSM_STATE), F32)],
        scratch_shapes=[pltpu.VMEM((step, TAIL + lc, SSM_XBC), F32)],
        compiler_params=_cparams(("arbitrary", "arbitrary")),
        name="ssd_mixer_state" if has_state else "ssd_mixer",
    )(*args)


def _xattn_kernel(q_ref, k_ref, v_ref, o_ref):
    for h in range(MEM_HEADS):
        sl = slice(h * MEM_HD, (h + 1) * MEM_HD)
        k = k_ref[0, :, sl]
        v = v_ref[0, :, sl]
        s = _bdot_nt(q_ref[0, :, sl], k) * (MEM_HD ** -0.5)
        s = s - jnp.max(s, axis=-1, keepdims=True)
        p = jnp.exp(s)
        p = p / jnp.sum(p, axis=-1, keepdims=True)
        o_ref[0, :, sl] = _bdot(p, v)


def _xattn_cache_kernel(q_ref, k_ref, v_ref, o_ref, *, tl):
    k2 = k_ref[...].reshape(MEM_LEN * MEM_HEADS, MEM_HD)
    v2 = v_ref[...].reshape(MEM_LEN * MEM_HEADS, MEM_HD)
    q4 = jnp.concatenate([q_ref[0, :, h * MEM_HD:(h + 1) * MEM_HD] for h in range(MEM_HEADS)], axis=0)
    s = _bdot_nt(q4, k2) * (MEM_HD ** -0.5)
    row_head = jnp.right_shift(lax.broadcasted_iota(jnp.int32, s.shape, 0), tl.bit_length() - 1)
    col_head = jnp.bitwise_and(lax.broadcasted_iota(jnp.int32, s.shape, 1), MEM_HEADS - 1)
    s = jnp.where(row_head == col_head, s, -jnp.inf)
    s = s - jnp.max(s, axis=-1, keepdims=True)
    p = jnp.exp(s)
    p = p / jnp.sum(p, axis=-1, keepdims=True)
    o4 = _bdot(p, v2)
    for h in range(MEM_HEADS):
        o_ref[0, :, h * MEM_HD:(h + 1) * MEM_HD] = o4[h * tl:(h + 1) * tl, :]


def xattn_cache(q3, row0, nseq, tl, cache_k, cache_v):
    assert tl & (tl - 1) == 0 and MEM_HEADS & (MEM_HEADS - 1) == 0
    kv_spec = pl.BlockSpec((None, None, MEM_LEN, MEM_HEADS, MEM_HD), lambda b: (0, b, 0, 0, 0))
    return pl.pallas_call(
        functools.partial(_xattn_cache_kernel, tl=tl),
        grid=(nseq,),
        in_specs=[pl.BlockSpec((1, tl, D_MODEL), lambda b: (row0 + b, 0, 0)), kv_spec, kv_spec],
        out_specs=pl.BlockSpec((1, tl, D_MODEL), lambda b: (b, 0, 0)),
        out_shape=jax.ShapeDtypeStruct((nseq, tl, D_MODEL), F32),
        compiler_params=_cparams(("arbitrary",)),
        name="xattn_cache",
    )(q3, cache_k, cache_v)


def xattn_core(q3, row0, nseq, ntile, tl, mem_k, mem_v, kv_col=(0, 0)):
    kv_specs = [pl.BlockSpec((1, MEM_LEN, D_MODEL), lambda b, t, c_=c_: (b, 0, c_)) for c_ in kv_col]
    return pl.pallas_call(
        _xattn_kernel,
        grid=(nseq, ntile),
        in_specs=[pl.BlockSpec((1, tl, D_MODEL), lambda b, t: (row0 + b * ntile + t, 0, 0))] + kv_specs,
        out_specs=pl.BlockSpec((1, tl, D_MODEL), lambda b, t: (b * ntile + t, 0, 0)),
        out_shape=jax.ShapeDtypeStruct((nseq * ntile, tl, D_MODEL), F32),
        compiler_params=_cparams(("arbitrary", "arbitrary")),
        name="xattn_core",
    )(q3, mem_k, mem_v)


def _top16_rows(s, break_ties):
    n = s.shape[0]
    rank = jnp.full(s.shape, NOT_TOP, F32)
    vals = []
    v = s
    for a in range(PEER_TOPK):
        m = jnp.max(v, axis=0, keepdims=True)
        hit = v == m
        if break_ties:
            iota = lax.broadcasted_iota(jnp.int32, s.shape, 0).astype(F32)
            hit = iota == jnp.min(jnp.where(hit, iota, float(n)), axis=0, keepdims=True)
        rank = jnp.where(hit, float(a), rank)
        v = jnp.where(hit, -jnp.inf, v)
        vals.append(m)
    return jnp.concatenate(vals, axis=0), rank


def _pair_top16(s0v, s1v):
    t = s0v.shape[1]
    iota = lax.broadcasted_iota(jnp.int32, (PEER_TOPK, t), 0).astype(F32)
    n = jnp.zeros((PEER_TOPK, t), F32)
    front = s0v + s1v[0:1, :]
    top = s0v[0:1, :] + s1v[0:1, :]
    z = jnp.zeros((1, t), F32)
    for _ in range(PEER_TOPK):
        m = jnp.max(front, axis=0, keepdims=True)
        a_star = jnp.min(jnp.where(front == m, iota, float(PEER_TOPK)), axis=0, keepdims=True)
        hit = iota == a_star
        z = z + jnp.exp(m - top)
        n = jnp.where(hit, n + 1.0, n)
        nxt = jnp.full((PEER_TOPK, t), -jnp.inf, F32)
        for b in range(1, PEER_TOPK):
            nxt = jnp.where(n == float(b), s1v[b:b + 1, :], nxt)
        front = jnp.where(hit, s0v + nxt, front)
    return n, z


def _peer_route_kernel(x_ref, g_ref, wq_ref, sk_ref, xn_ref, e0_ref, ni_ref, e1_ref, rj_ref, q_ref):
    xn = _rmsnorm_rows(x_ref[...], g_ref[...]).astype(BF16)
    xn_ref[...] = xn
    q_ref[...] = lax.dot_general(wq_ref[...], xn, (((1,), (1,)), ((), ())),
                                 preferred_element_type=F32).astype(BF16)
    tt = x_ref.shape[0]

    def route(break_ties):
        bad = jnp.zeros((1, LANES), F32)
        for lb in range(tt // LANES):
            ls = slice(lb * LANES, (lb + 1) * LANES)
            for h in range(PEER_HEADS):
                halves = []
                for c in range(2):
                    r0 = (2 * h + c) * PEER_HALF
                    s = jnp.dot(sk_ref[c], q_ref[r0:r0 + PEER_HALF, ls],
                                preferred_element_type=F32)
                    vals, rank = _top16_rows(s, break_ties)
                    halves.append((s, vals, rank))
                (s0, s0v, rank0), (s1, s1v, rank1) = halves
                n, z = _pair_top16(s0v, s1v)
                in0 = rank0 < float(PEER_TOPK)
                in1 = rank1 < float(PEER_TOPK)
                if not break_ties:
                    for inside in (in0, in1):
                        count = jnp.sum(jnp.where(inside, 1.0, 0.0), axis=0, keepdims=True)
                        bad = jnp.maximum(bad, jnp.abs(count - float(PEER_TOPK)))
                e0 = jnp.where(in0, jnp.exp(jnp.where(in0, s0 - s0v[0:1, :], 0.0)), 0.0) / z
                e1 = jnp.where(in1, jnp.exp(jnp.where(in1, s1 - s1v[0:1, :], 0.0)), 0.0)
                ni = jnp.zeros_like(rank0)
                for a in range(PEER_TOPK):
                    ni = jnp.where(rank0 == float(a), n[a:a + 1, :], ni)
                e0_ref[h, :, ls] = e0
                ni_ref[h, :, ls] = ni
                e1_ref[h, :, ls] = e1.astype(BF16)
                rj_ref[h, :, ls] = rank1.astype(BF16)
        return bad

    has_ties = jnp.max(route(break_ties=False)) > 0.0

    @pl.when(has_ties)
    def _():
        route(break_ties=True)


def peer_route(x, g, wq_t, sub_keys, *, tt):
    n = x.shape[0]
    fspec = pl.BlockSpec((PEER_HEADS, PEER_NKEYS, tt), lambda i: (0, 0, i))

    def fac(dtype):
        return jax.ShapeDtypeStruct((PEER_HEADS, PEER_NKEYS, n), dtype)

    return pl.pallas_call(
        _peer_route_kernel,
        grid=(n // tt,),
        in_specs=[pl.BlockSpec((tt, D_MODEL), lambda i: (i, 0)),
                  pl.BlockSpec((1, D_MODEL), lambda i: (0, 0)),
                  pl.BlockSpec(wq_t.shape, lambda i: (0, 0)),
                  pl.BlockSpec(sub_keys.shape, lambda i: (0, 0, 0))],
        out_specs=[pl.BlockSpec((tt, D_MODEL), lambda i: (i, 0)), fspec, fspec, fspec, fspec],
        out_shape=[jax.ShapeDtypeStruct((n, D_MODEL), BF16), fac(F32), fac(F32), fac(BF16), fac(BF16)],
        scratch_shapes=[pltpu.VMEM((PEER_HEADS * 2 * PEER_HALF, tt), BF16)],
        compiler_params=_cparams(("arbitrary",)),
        name="peer_route",
    )(x, g.reshape(1, D_MODEL), wq_t, sub_keys)


def _peer_dense_kernel(xn_ref, u_ref, vt_ref, e0_ref, ni_ref, e1_ref, rj_ref, x_ref, gf_ref,
                       yp_ref, ys_ref, acc_ref, a_ref, p_ref, *, te, np_tiles):
    i_tok = pl.program_id(0)
    e = pl.program_id(1)

    @pl.when(e == 0)
    def _():
        acc_ref[...] = jnp.zeros_like(acc_ref)

    zero = jnp.zeros((), BF16)
    rows_per_sub = PEER_SUB // PEER_NKEYS

    def pre_activations(s):
        rs = slice(s * PEER_SUB, (s + 1) * PEER_SUB)
        a_ref[rs, :] = lax.dot_general(u_ref[rs, :], xn_ref[...], (((1,), (1,)), ((), ())),
                                       preferred_element_type=F32)

    def weighted_activations(s):
        for r in range(s * rows_per_sub, (s + 1) * rows_per_sub):
            i = e * (te // PEER_NKEYS) + r
            w = None
            for h in range(PEER_HEADS):
                e0r = e0_ref[h, pl.ds(i, 1), :].astype(BF16)
                nir = ni_ref[h, pl.ds(i, 1), :].astype(BF16)
                wh = jnp.where(rj_ref[h] < nir, e0r * e1_ref[h], zero)
                w = wh if w is None else w + wh
            a = a_ref[r * PEER_NKEYS:(r + 1) * PEER_NKEYS, :]
            act = 0.5 * a * (1.0 + lax.erf(a * (2.0 ** -0.5)))
            p_ref[r * PEER_NKEYS:(r + 1) * PEER_NKEYS, :] = w * act.astype(BF16)

    nsub = te // PEER_SUB
    pre_activations(0)
    for s in range(nsub):
        if s + 1 < nsub:
            pre_activations(s + 1)
        weighted_activations(s)
        rs = slice(s * PEER_SUB, (s + 1) * PEER_SUB)
        acc_ref[...] += jnp.dot(vt_ref[:, rs], p_ref[rs, :], preferred_element_type=F32)

    @pl.when(e == pl.num_programs(1) - 1)
    def _():
        y = _rmsnorm_rows(x_ref[...] + jnp.transpose(acc_ref[...]), gf_ref[...])

        @pl.when(i_tok < np_tiles)
        def _():
            yp_ref[...] = y

        @pl.when(i_tok >= np_tiles)
        def _():
            ys_ref[...] = y


def peer_dense(xn, u, v_t, e0, ni, e1, rj, x, g_final, n_p, *, tt, te):
    n = xn.shape[0]
    np_tiles = n_p // tt
    fspec = pl.BlockSpec((PEER_HEADS, PEER_NKEYS, tt), lambda i, e: (0, 0, i))
    return pl.pallas_call(
        functools.partial(_peer_dense_kernel, te=te, np_tiles=np_tiles),
        grid=(n // tt, PEER_EXPERTS // te),
        in_specs=[pl.BlockSpec((tt, D_MODEL), lambda i, e: (i, 0)),
                  pl.BlockSpec((te, D_MODEL), lambda i, e: (e, 0)),
                  pl.BlockSpec((D_MODEL, te), lambda i, e: (0, e)),
                  fspec, fspec, fspec, fspec,
                  pl.BlockSpec((tt, D_MODEL), lambda i, e: (i, 0)),
                  pl.BlockSpec((1, D_MODEL), lambda i, e: (0, 0))],
        out_specs=[pl.BlockSpec((tt, D_MODEL), lambda i, e: (jnp.minimum(i, np_tiles - 1), 0)),
                   pl.BlockSpec((tt, D_MODEL), lambda i, e: (jnp.maximum(i - np_tiles, 0), 0))],
        out_shape=[jax.ShapeDtypeStruct((n_p, D_MODEL), F32),
                   jax.ShapeDtypeStruct((n - n_p, D_MODEL), F32)],
        scratch_shapes=[pltpu.VMEM((D_MODEL, tt), F32),
                        pltpu.VMEM((te, tt), F32),
                        pltpu.VMEM((te, tt), BF16)],
        compiler_params=_cparams(("arbitrary", "arbitrary")),
        name="peer_dense",
    )(xn, u, v_t, e0, ni, e1, rj, x, g_final.reshape(1, D_MODEL))


def _lane_row(vec, start, width=GATE_W):
    return jnp.zeros((1, width), F32).at[0, start:start + vec.shape[0]].set(vec.astype(F32))


def _rep_matrix(first_row, heads, lanes_per_head, rows=GATE_W):
    r = jnp.arange(rows)[:, None]
    c = jnp.arange(heads * lanes_per_head)[None, :]
    return ((c // lanes_per_head) + first_row == r).astype(BF16)


def _pad_tail(buf):
    return jnp.pad(buf, ((0, 0), (TAIL - (CONV_K - 1), 0), (0, 0)))


def kernel(x_prompt, x_sample, state_dn_conv, state_dn, state_ssm_conv, state_ssm, cache_mem_k, cache_mem_v,
           mem_prompt, g_mix, w_in, dn_conv_w, dn_A_log, dn_dt_bias, dn_norm_w, ssm_conv_w, ssm_conv_b,
           ssm_A_log, ssm_dt_bias, ssm_D, ssm_norm_w, w_out, g_xattn, g_mem, w_xq, w_mkv, w_xo, g_ffn, w_pq,
           peer_sub_keys, peer_u, peer_v, g_final):
    depth = g_mix.shape[0]
    assert depth == 1
    bp, lp, d = x_prompt.shape
    bs, ls, _ = x_sample.shape
    n_p, n_s = bp * lp, bs * ls
    n = n_p + n_s
    ncp = lp // CHUNK
    tm = 512
    x = (x_prompt.reshape(n_p, d), x_sample.reshape(n_s, d))

    wi = w_in[0]
    o_xbc = DN_QKV
    o_zdn = o_xbc + SSM_XBC
    o_zssm = o_zdn + DN_WIDTH
    o_small = o_zssm + SSM_WIDTH
    n_small = 2 * DN_HEADS + SSM_HEADS
    w_cat = jnp.concatenate([
        wi[:, :DN_QKV], wi[:, o_xbc:o_xbc + SSM_XBC],
        wi[:, o_small:o_small + n_small], jnp.zeros((d, COL_ZDN - COL_GATE - n_small), F32),
        wi[:, o_zdn:o_zdn + DN_WIDTH], wi[:, o_zssm:o_zssm + SSM_WIDTH]], axis=1).astype(BF16)

    proj = norm_matmul(x, g_mix[0], w_cat, tm=2 * tm, tn=1024)
    proj_c = proj.reshape(n // CHUNK, CHUNK, PROJ_COLS)
    proj_s = proj.reshape(n // ls, ls, PROJ_COLS)

    gdn_params = (dn_conv_w[0],
                  jnp.concatenate([_lane_row(dn_A_log[0], DN_HEADS), _lane_row(dn_dt_bias[0], DN_HEADS)], axis=0),
                  dn_norm_w[0].reshape(1, DN_DK),
                  _rep_matrix(0, DN_HEADS, DN_DK), _rep_matrix(DN_HEADS, DN_HEADS, DN_DK))
    ssd_params = (ssm_conv_w[0], ssm_conv_b[0].reshape(1, SSM_XBC),
                  jnp.concatenate([_lane_row(ssm_A_log[0], 2 * DN_HEADS), _lane_row(ssm_dt_bias[0], 2 * DN_HEADS)],
                                  axis=0),
                  jnp.repeat(ssm_D[0], SSM_HEADDIM).reshape(1, SSM_WIDTH),
                  ssm_norm_w[0].reshape(1, SSM_WIDTH),
                  _rep_matrix(2 * DN_HEADS, SSM_HEADS, SSM_HEADDIM), _rep_matrix(2 * DN_HEADS, SSM_HEADS, LANES))

    odn_p, p_dn = gdn_mixer(proj_c, 0, bp, ncp, CHUNK, 2, 1, gdn_params)
    odn_s, s_dn = gdn_mixer(proj_s, n_p // ls, bs, 1, ls, 1, 4, gdn_params,
                            buf=_pad_tail(state_dn_conv[0]), s0=state_dn[0])
    ossm_p, p_ss = ssd_mixer(proj_c, 0, bp, ncp, CHUNK, 2, 1, ssd_params)
    ossm_s, s_ss = ssd_mixer(proj_s, n_p // ls, bs, 1, ls, 1, 4, ssd_params,
                             buf=_pad_tail(state_ssm_conv[0]),
                             s0=state_ssm[0].reshape(bs, SSM_PAIRS, PAIR_W, SSM_STATE))
    wo = w_out[0].astype(BF16)
    x1 = mixer_out((odn_p.reshape(n_p, DN_WIDTH), odn_s.reshape(n_s, DN_WIDTH)),
                   (ossm_p.reshape(n_p, SSM_WIDTH), ossm_s.reshape(n_s, SSM_WIDTH)),
                   wo[:DN_WIDTH], wo[DN_WIDTH:], x, tm=tm)

    last_c = slice(ncp - 1, bp * ncp, ncp)
    hist_c = slice(CHUNK - (CONV_K - 1), CHUNK)
    hist_s = slice(ls - (CONV_K - 1), ls)
    p_dnc = proj_c[last_c, hist_c, COL_QKV:COL_QKV + DN_QKV]
    p_sc = proj_c[last_c, hist_c, COL_XBC:COL_XBC + SSM_XBC]
    s_dnc = proj_s[n_p // ls:, hist_s, COL_QKV:COL_QKV + DN_QKV]
    s_sc = proj_s[n_p // ls:, hist_s, COL_XBC:COL_XBC + SSM_XBC]

    mkv = norm_matmul(mem_prompt.reshape(bp * MEM_LEN, d), g_mem[0], w_mkv[0].astype(BF16), tm=tm, tn=1024)
    mkv3 = mkv.reshape(bp, MEM_LEN, 2 * d)
    q = norm_matmul(x1, g_xattn[0], w_xq[0].astype(BF16), tm=tm, tn=1024)
    tl = 512
    a_p = xattn_core(q.reshape(n // tl, tl, d), 0, bp, lp // tl, tl, mkv3, mkv3, kv_col=(0, 1))
    a_s = xattn_cache(q.reshape(n // ls, ls, d), n_p // ls, bs, ls, cache_mem_k, cache_mem_v)
    x2 = matmul_res((a_p.reshape(n_p, d), a_s.reshape(n_s, d)), w_xo[0].astype(BF16), x1, tm=tm)

    xn, e0, ni, e1, rj = peer_route(x2, g_ffn[0], jnp.transpose(w_pq[0]).astype(BF16),
                                    peer_sub_keys[0].astype(BF16), tt=256)
    y_p, y_s = peer_dense(xn, peer_u[0].astype(BF16), jnp.transpose(peer_v[0]).astype(BF16),
                          e0, ni, e1, rj, x2, g_final, n_p, tt=512, te=2048)

    def heads(m):
        return m.reshape(bp, MEM_LEN, MEM_HEADS, MEM_HD)[None]

    return (y_p.reshape(bp, lp, d), y_s.reshape(bs, ls, d),
            p_dnc[None], p_dn[None], p_sc[None], p_ss.reshape(bp, SSM_HEADS, SSM_HEADDIM, SSM_STATE)[None],
            heads(mkv3[:, :, :d]), heads(mkv3[:, :, d:]),
            s_dnc[None], s_dn[None], s_sc[None], s_ss.reshape(bs, SSM_HEADS, SSM_HEADDIM, SSM_STATE)[None])
```

```python
import functools

import jax
import jax.numpy as jnp
from jax import lax
from jax.experimental import pallas as pl
from jax.experimental.pallas import tpu as pltpu

F32 = jnp.float32
BF16 = jnp.bfloat16
EPS = 1e-6

D_MODEL = 1024
CONV_K = 4
CHUNK = 64
DN_HEADS = 8
DN_DK = 128
DN_WIDTH = 1024
DN_QKV = 3072
SSM_HEADS = 16
SSM_HEADDIM = 64
SSM_STATE = 128
SSM_WIDTH = 1024
SSM_XBC = 1536
MEM_LEN = 256
MEM_HEADS = 4
MEM_HD = 256
PEER_HEADS = 8
PEER_NKEYS = 128
PEER_EXPERTS = PEER_NKEYS * PEER_NKEYS
PEER_TOPK = 16
PEER_HALF = 128

COL_QKV = 0
COL_XBC = 3072
COL_GATE = 4608
COL_ZDN = 5120
COL_ZSSM = 6144
PROJ_COLS = 7168
GATE_W = 128
TAIL = 8
INV_BLOCK = 16
NOT_TOP = 99.0
LANES = 128
BF16_SUBLANES = 16
PEER_SUB = 1024

VMEM_LIMIT = 56 * 1024 * 1024


def _cparams(sem):
    return pltpu.CompilerParams(dimension_semantics=sem, vmem_limit_bytes=VMEM_LIMIT)


def _bdot(a, b):
    return jnp.dot(a.astype(BF16), b.astype(BF16), preferred_element_type=F32)


def _bdot_nt(a, b):
    return lax.dot_general(a.astype(BF16), b.astype(BF16), (((1,), (1,)), ((), ())),
                           preferred_element_type=F32)


def _bdot_tn(a, b):
    return lax.dot_general(a.astype(BF16), b.astype(BF16), (((0,), (0,)), ((), ())),
                           preferred_element_type=F32)


def _split3(x):
    hi = x.astype(BF16)
    r = x - hi.astype(F32)
    mid = r.astype(BF16)
    lo = (r - mid.astype(F32)).astype(BF16)
    return hi, mid, lo


def _dot_sel_rhs(x, sel):
    hi, mid, lo = _split3(x)
    d = functools.partial(jnp.dot, preferred_element_type=F32)
    return d(hi, sel) + d(mid, sel) + d(lo, sel)


def _dot_sel_lhs(sel, x):
    hi, mid, lo = _split3(x)
    d = functools.partial(jnp.dot, preferred_element_type=F32)
    return d(sel, hi) + d(sel, mid) + d(sel, lo)


def _silu(x):
    return x * jax.nn.sigmoid(x)


def _softplus(x):
    return jnp.maximum(x, 0.0) + jnp.log1p(jnp.exp(-jnp.abs(x)))


def _unit_lower_inverse(lms, n):
    row = lax.broadcasted_iota(jnp.int32, (n, n), 0)
    col = lax.broadcasted_iota(jnp.int32, (n, n), 1)
    eye = (row == col).astype(F32)

    def nilpotent_inverse(xs, index):
        invs = [eye - x for x in xs]
        ps = xs
        k = 2
        while k < index:
            ps = [_bdot(p, p) for p in ps]
            invs = [inv + _bdot(inv, p) for inv, p in zip(invs, ps)]
            k *= 2
        return invs

    if n <= INV_BLOCK:
        return nilpotent_inverse(lms, n)
    shift = INV_BLOCK.bit_length() - 1
    same = jnp.right_shift(row, shift) == jnp.right_shift(col, shift)
    dinvs = nilpotent_inverse([jnp.where(same, lm, 0.0) for lm in lms], INV_BLOCK)
    fs = [_bdot(dinv, jnp.where(same, 0.0, lm)) for dinv, lm in zip(dinvs, lms)]
    finvs = nilpotent_inverse(fs, n // INV_BLOCK)
    return [_bdot(finv, dinv) for finv, dinv in zip(finvs, dinvs)]


def _causal_conv(x, xp_ref, w, lc):
    xp_ref[TAIL:TAIL + lc, :] = x
    y = x * w[3:4, :]
    for s in range(1, CONV_K):
        y = y + xp_ref[TAIL - s:TAIL - s + lc, :] * w[CONV_K - 1 - s:CONV_K - s, :]
    xp_ref[0:TAIL, :] = xp_ref[lc:lc + TAIL, :]
    return y


def _decay_matrix(col, row, lower_incl):
    diff = jnp.where(lower_incl, col - row, 0.0)
    return jnp.where(lower_incl, jnp.exp(diff), 0.0)


def _pair_specs(tm, k, np_tiles, two_axes):
    if two_axes:
        return [pl.BlockSpec((tm, k), lambda i, j: (jnp.minimum(i, np_tiles - 1), 0)),
                pl.BlockSpec((tm, k), lambda i, j: (jnp.maximum(i - np_tiles, 0), 0))]
    return [pl.BlockSpec((tm, k), lambda i: (jnp.minimum(i, np_tiles - 1), 0)),
            pl.BlockSpec((tm, k), lambda i: (jnp.maximum(i - np_tiles, 0), 0))]


def _pick(is_prompt, p_ref, s_ref):
    return jnp.where(is_prompt, p_ref[...], s_ref[...])


def _rmsnorm_rows(x, g):
    ms = jnp.mean(x * x, axis=-1, keepdims=True)
    return x * lax.rsqrt(ms + EPS) * g


def _norm_matmul_kernel(*refs, np_tiles):
    if np_tiles is None:
        x_ref, g_ref, w_ref, o_ref, xn_ref = refs
    else:
        xp_ref, xs_ref, g_ref, w_ref, o_ref, xn_ref = refs
        is_prompt = pl.program_id(0) < np_tiles

    @pl.when(pl.program_id(1) == 0)
    def _():
        x = x_ref[...] if np_tiles is None else _pick(is_prompt, xp_ref, xs_ref)
        xn_ref[...] = _rmsnorm_rows(x, g_ref[...]).astype(BF16)

    o_ref[...] = jnp.dot(xn_ref[...], w_ref[...], preferred_element_type=F32).astype(o_ref.dtype)


def norm_matmul(x, g, w, *, tm, tn, out_dtype=F32):
    pair = isinstance(x, tuple)
    n = sum(a.shape[0] for a in x) if pair else x.shape[0]
    k, m = w.shape
    assert n % tm == 0 and m % tn == 0
    if pair:
        assert x[0].shape[0] % tm == 0
        np_tiles = x[0].shape[0] // tm
        x_specs = _pair_specs(tm, k, np_tiles, True)
        x_args = list(x)
    else:
        np_tiles = None
        x_specs = [pl.BlockSpec((tm, k), lambda i, j: (i, 0))]
        x_args = [x]
    return pl.pallas_call(
        functools.partial(_norm_matmul_kernel, np_tiles=np_tiles),
        grid=(n // tm, m // tn),
        in_specs=x_specs + [pl.BlockSpec((1, k), lambda i, j: (0, 0)),
                            pl.BlockSpec((k, tn), lambda i, j: (0, j))],
        out_specs=pl.BlockSpec((tm, tn), lambda i, j: (i, j)),
        out_shape=jax.ShapeDtypeStruct((n, m), out_dtype),
        scratch_shapes=[pltpu.VMEM((tm, k), BF16)],
        compiler_params=_cparams(("arbitrary", "arbitrary")),
        name="norm_matmul",
    )(*x_args, g.reshape(1, k), w)


def _mixer_out_kernel(dp_ref, ds_ref, sp_ref, ss_ref, w1_ref, w2_ref, xp_ref, xs_ref, o_ref, *, np_tiles):
    is_prompt = pl.program_id(0) < np_tiles
    a1 = _pick(is_prompt, dp_ref, ds_ref).astype(BF16)
    a2 = _pick(is_prompt, sp_ref, ss_ref).astype(BF16)
    acc = jnp.dot(a1, w1_ref[...], preferred_element_type=F32)
    acc = acc + jnp.dot(a2, w2_ref[...], preferred_element_type=F32)
    o_ref[...] = _pick(is_prompt, xp_ref, xs_ref) + acc


def mixer_out(o_dn, o_ssm, w1, w2, x, *, tm):
    n = x[0].shape[0] + x[1].shape[0]
    k = w1.shape[0]
    m = w1.shape[1]
    np_tiles = x[0].shape[0] // tm
    wspec = pl.BlockSpec((k, m), lambda i: (0, 0))
    return pl.pallas_call(
        functools.partial(_mixer_out_kernel, np_tiles=np_tiles),
        grid=(n // tm,),
        in_specs=(_pair_specs(tm, k, np_tiles, False) + _pair_specs(tm, k, np_tiles, False)
                  + [wspec, wspec] + _pair_specs(tm, m, np_tiles, False)),
        out_specs=pl.BlockSpec((tm, m), lambda i: (i, 0)),
        out_shape=jax.ShapeDtypeStruct((n, m), F32),
        compiler_params=_cparams(("arbitrary",)),
        name="mixer_out",
    )(*o_dn, *o_ssm, w1, w2, *x)


def _matmul_res_kernel(ap_ref, as_ref, w_ref, r_ref, o_ref, *, np_tiles):
    a = _pick(pl.program_id(0) < np_tiles, ap_ref, as_ref).astype(BF16)
    o_ref[...] = r_ref[...] + jnp.dot(a, w_ref[...], preferred_element_type=F32)


def matmul_res(a, w, res, *, tm):
    n, m = res.shape
    k = w.shape[0]
    np_tiles = a[0].shape[0] // tm
    return pl.pallas_call(
        functools.partial(_matmul_res_kernel, np_tiles=np_tiles),
        grid=(n // tm,),
        in_specs=_pair_specs(tm, k, np_tiles, False) + [pl.BlockSpec((k, m), lambda i: (0, 0)),
                                                        pl.BlockSpec((tm, m), lambda i: (i, 0))],
        out_specs=pl.BlockSpec((tm, m), lambda i: (i, 0)),
        out_shape=jax.ShapeDtypeStruct((n, m), F32),
        compiler_params=_cparams(("arbitrary",)),
        name="matmul_res",
    )(*a, w, res)


def _gdn_kernel(*refs, lc, ng, nbk, has_state):
    refs = list(refs)

    def take(k):
        out = refs[:k]
        del refs[:k]
        return out

    qkv_refs, gate_refs, z_refs = take(ng), take(ng), take(ng)
    if has_state:
        buf_ref, s0_ref = take(2)
    cw_ref, gp_ref, nw_ref, repb_ref, repg_ref, o_ref, s_ref, xp_ref = take(8)
    nseq = ng * nbk
    where = [divmod(j, nbk) for j in range(nseq)]

    @pl.when(pl.program_id(1) == 0)
    def _():
        if has_state:
            xp_ref[:, 0:TAIL, :] = buf_ref[...]
            s_ref[...] = s0_ref[...]
        else:
            xp_ref[:, 0:TAIL, :] = jnp.zeros((nseq, TAIL, DN_QKV), F32)
            s_ref[...] = jnp.zeros(s_ref.shape, F32)

    row = lax.broadcasted_iota(jnp.int32, (lc, lc), 0)
    col = lax.broadcasted_iota(jnp.int32, (lc, lc), 1)
    incl = row >= col
    strict = row > col
    tri = incl.astype(BF16)

    seqs = []
    for j, (g_, k_) in enumerate(where):
        qkv = _silu(_causal_conv(qkv_refs[g_][k_], xp_ref.at[j], cw_ref[...], lc))
        gate = gate_refs[g_][k_]
        beta_all = jax.nn.sigmoid(gate)
        g_all = -jnp.exp(gp_ref[0:1, :]) * _softplus(gate + gp_ref[1:2, :])
        gc = _dot_sel_lhs(tri, g_all)
        gc_b = _dot_sel_rhs(gc, repg_ref[...])
        gc_last_b = gc_b[lc - 1:lc, :]
        seqs.append(dict(qkv=qkv, gc_t=jnp.transpose(gc), gc_b=gc_b,
                         beta_b=_dot_sel_rhs(beta_all, repb_ref[...]),
                         egc_b=jnp.exp(gc_b), ekd_b=jnp.exp(gc_last_b - gc_b), egt_b=jnp.exp(gc_last_b)))

    units = [(j, h) for j in range(nseq) for h in range(DN_HEADS)]
    qs, ks, vbs, kbegs, decays = [], [], [], [], []
    for j, h in units:
        sq = seqs[j]
        sl = slice(h * DN_DK, (h + 1) * DN_DK)
        qh = sq["qkv"][:, h * DN_DK:(h + 1) * DN_DK]
        kh = sq["qkv"][:, DN_WIDTH + h * DN_DK:DN_WIDTH + (h + 1) * DN_DK]
        vh = sq["qkv"][:, 2 * DN_WIDTH + h * DN_DK:2 * DN_WIDTH + (h + 1) * DN_DK]
        qh = qh * lax.rsqrt(jnp.sum(qh * qh, axis=-1, keepdims=True) + EPS) * (DN_DK ** -0.5)
        kh = kh * lax.rsqrt(jnp.sum(kh * kh, axis=-1, keepdims=True) + EPS)
        bh = sq["beta_b"][:, sl]
        qs.append(qh)
        ks.append(kh)
        vbs.append(vh * bh)
        kbegs.append((kh * bh, sq["egc_b"][:, sl]))
        decays.append(_decay_matrix(sq["gc_b"][:, h * DN_DK:h * DN_DK + lc],
                                    sq["gc_t"][DN_HEADS + h:DN_HEADS + h + 1, :], incl))
    lowers = [jnp.where(strict, _bdot_nt(kb, kh) * dec, 0.0)
              for (kb, _), kh, dec in zip(kbegs, ks, decays)]
    attns = [_bdot_nt(qh, kh) * dec for qh, kh, dec in zip(qs, ks, decays)]
    tinvs = _unit_lower_inverse(lowers, lc)
    us = [_bdot(t, vb) for t, vb in zip(tinvs, vbs)]
    ws = [_bdot(t, kb * eg) for t, (kb, eg) in zip(tinvs, kbegs)]
    states = [s_ref[j, h] for j, h in units]
    v_news = [u - _bdot(w, s) for u, w, s in zip(us, ws, states)]
    os_ = [_bdot(qh * eg, s) + _bdot(a, vn)
           for qh, (_, eg), s, a, vn in zip(qs, kbegs, states, attns, v_news)]
    for (j, h), kh, s, vn, o in zip(units, ks, states, v_news, os_):
        g_, k_ = where[j]
        sl = slice(h * DN_DK, (h + 1) * DN_DK)
        sq = seqs[j]
        s_ref[j, h] = s * sq["egt_b"][:, sl] + _bdot_tn(kh * sq["ekd_b"][:, sl], vn)
        o = o * lax.rsqrt(jnp.mean(o * o, axis=-1, keepdims=True) + EPS) * nw_ref[...]
        o_ref[j, 0, :, sl] = o * _silu(z_refs[g_][k_, :, sl])


def gdn_mixer(proj3, row0, nseq, nchunk, lc, ng, nbk, params, buf=None, s0=None):
    has_state = buf is not None
    step = ng * nbk
    assert nseq % step == 0 and row0 % nbk == 0 and (nbk == 1 or nchunk == 1)

    def proj_specs(width, col_block):
        return [pl.BlockSpec((nbk, lc, width),
                             lambda b, c, g_=g_: (row0 // nbk + (b * ng + g_) * nchunk + c, 0, col_block))
                for g_ in range(ng)]

    in_specs = (proj_specs(DN_QKV, COL_QKV // DN_QKV) + proj_specs(GATE_W, COL_GATE // GATE_W)
                + proj_specs(DN_WIDTH, COL_ZDN // DN_WIDTH))
    args = [proj3] * (3 * ng)
    state_spec = pl.BlockSpec((step, DN_HEADS, DN_DK, DN_DK), lambda b, c: (b, 0, 0, 0))
    if has_state:
        in_specs += [pl.BlockSpec((step, TAIL, DN_QKV), lambda b, c: (b, 0, 0)), state_spec]
        args += [buf, s0]
    for prm in params:
        in_specs.append(pl.BlockSpec(prm.shape, lambda b, c: (0, 0)))
        args.append(prm)
    return pl.pallas_call(
        functools.partial(_gdn_kernel, lc=lc, ng=ng, nbk=nbk, has_state=has_state),
        grid=(nseq // step, nchunk),
        in_specs=in_specs,
        out_specs=[pl.BlockSpec((step, 1, lc, DN_WIDTH), lambda b, c: (b, c, 0, 0)), state_spec],
        out_shape=[jax.ShapeDtypeStruct((nseq, nchunk, lc, DN_WIDTH), F32),
                   jax.ShapeDtypeStruct((nseq, DN_HEADS, DN_DK, DN_DK), F32)],
        scratch_shapes=[pltpu.VMEM((step, TAIL + lc, DN_QKV), F32)],
        compiler_params=_cparams(("arbitrary", "arbitrary")),
        name="gdn_mixer_state" if has_state else "gdn_mixer",
    )(*args)


SSM_PAIRS = SSM_HEADS // 2
PAIR_W = 2 * SSM_HEADDIM
GROUP_W = SSM_WIDTH // 2


def _ssd_kernel(*refs, lc, ng, nbk, has_state):
    refs = list(refs)

    def take(k):
        out = refs[:k]
        del refs[:k]
        return out

    xbc_refs, gate_refs, z_refs = take(ng), take(ng), take(ng)
    if has_state:
        buf_ref, s0_ref = take(2)
    (cw_ref, cb_ref, gp_ref, dd_ref, nw_ref, reps_ref, repw_ref, o_ref, s_ref, xp_ref) = take(10)
    nseq = ng * nbk
    where = [divmod(j, nbk) for j in range(nseq)]

    @pl.when(pl.program_id(1) == 0)
    def _():
        if has_state:
            xp_ref[:, 0:TAIL, :] = buf_ref[...]
            s_ref[...] = s0_ref[...]
        else:
            xp_ref[:, 0:TAIL, :] = jnp.zeros((nseq, TAIL, SSM_XBC), F32)
            s_ref[...] = jnp.zeros(s_ref.shape, F32)

    row = lax.broadcasted_iota(jnp.int32, (lc, lc), 0)
    col = lax.broadcasted_iota(jnp.int32, (lc, lc), 1)
    incl = row >= col
    tri = incl.astype(BF16)
    first_head = lax.broadcasted_iota(jnp.int32, (lc, PAIR_W), 1) < SSM_HEADDIM

    seqs = []
    for j, (g_, k_) in enumerate(where):
        xbc = _silu(_causal_conv(xbc_refs[g_][k_], xp_ref.at[j], cw_ref[...], lc) + cb_ref[...])
        xs = xbc[:, :SSM_WIDTH]
        dt = _softplus(gate_refs[g_][k_] + gp_ref[1:2, :])
        acs = _dot_sel_lhs(tri, dt * (-jnp.exp(gp_ref[0:1, :])))
        acs_b = _dot_sel_rhs(acs, reps_ref[...])
        acs_w = _dot_sel_rhs(acs, repw_ref[...])
        seqs.append(dict(xbc=xbc, xs=xs, acs_t=jnp.transpose(acs), acs_w=acs_w,
                         xd=xs * _dot_sel_rhs(dt, reps_ref[...]),
                         eacs_b=jnp.exp(acs_b), ends_b=jnp.exp(acs_b[lc - 1:lc, :] - acs_b),
                         ecd_w=jnp.exp(acs_w[lc - 1:lc, :])))

    def bmat(sq, g):
        return sq["xbc"][:, SSM_WIDTH + g * SSM_STATE:SSM_WIDTH + (g + 1) * SSM_STATE]

    def cmat(sq, g):
        o = SSM_WIDTH + 2 * SSM_STATE
        return sq["xbc"][:, o + g * SSM_STATE:o + (g + 1) * SSM_STATE]

    cbs = [[_bdot_nt(cmat(sq, g), bmat(sq, g)) for g in range(2)] for sq in seqs]
    units = [(j, p) for j in range(nseq) for p in range(SSM_PAIRS)]
    lmats = [[_decay_matrix(seqs[j]["acs_w"][:, hh * LANES:hh * LANES + lc],
                            seqs[j]["acs_t"][2 * DN_HEADS + hh:2 * DN_HEADS + hh + 1, :], incl)
              for hh in (2 * p, 2 * p + 1)] for j, p in units]
    prevs = [s_ref[j, p] for j, p in units]
    y_diag, y_off, sts = [], [], []
    for (j, p), lm, prev in zip(units, lmats, prevs):
        sq = seqs[j]
        g = p // (SSM_PAIRS // 2)
        psl = slice(p * PAIR_W, (p + 1) * PAIR_W)
        xd_p = sq["xd"][:, psl]
        y_diag.append([_bdot(cbs[j][g] * lm[0], xd_p), _bdot(cbs[j][g] * lm[1], xd_p)])
        y_off.append(_bdot_nt(cmat(sq, g), prev))
        sts.append(_bdot_tn(xd_p * sq["ends_b"][:, psl], bmat(sq, g)))
    ys = []
    ssq = [[None, None] for _ in range(nseq)]
    for (j, p), yd, yo, st, prev in zip(units, y_diag, y_off, sts, prevs):
        sq = seqs[j]
        g_, k_ = where[j]
        g = p // (SSM_PAIRS // 2)
        psl = slice(p * PAIR_W, (p + 1) * PAIR_W)
        cd = jnp.concatenate(
            [jnp.broadcast_to(sq["ecd_w"][:, (2 * p) * LANES:(2 * p + 1) * LANES], (SSM_HEADDIM, SSM_STATE)),
             jnp.broadcast_to(sq["ecd_w"][:, (2 * p + 1) * LANES:(2 * p + 2) * LANES], (SSM_HEADDIM, SSM_STATE))],
            axis=0)
        s_ref[j, p] = prev * cd + st
        y = jnp.where(first_head, yd[0], yd[1]) + yo * sq["eacs_b"][:, psl] + dd_ref[:, psl] * sq["xs"][:, psl]
        y = y * _silu(z_refs[g_][k_, :, psl])
        s2 = jnp.sum(y * y, axis=-1, keepdims=True)
        ssq[j][g] = s2 if ssq[j][g] is None else ssq[j][g] + s2
        ys.append(y)
    for (j, p), y in zip(units, ys):
        g = p // (SSM_PAIRS // 2)
        psl = slice(p * PAIR_W, (p + 1) * PAIR_W)
        o_ref[j, 0, :, psl] = y * lax.rsqrt(ssq[j][g] * (1.0 / GROUP_W) + EPS) * nw_ref[:, psl]


def ssd_mixer(proj3, row0, nseq, nchunk, lc, ng, nbk, params, buf=None, s0=None):
    has_state = buf is not None
    step = ng * nbk
    assert nseq % step == 0 and row0 % nbk == 0 and (nbk == 1 or nchunk == 1)

    def proj_specs(width, col_block):
        return [pl.BlockSpec((nbk, lc, width),
                             lambda b, c, g_=g_: (row0 // nbk + (b * ng + g_) * nchunk + c, 0, col_block))
                for g_ in range(ng)]

    in_specs = (proj_specs(SSM_XBC, COL_XBC // SSM_XBC) + proj_specs(GATE_W, COL_GATE // GATE_W)
                + proj_specs(SSM_WIDTH, COL_ZSSM // SSM_WIDTH))
    args = [proj3] * (3 * ng)
    state_spec = pl.BlockSpec((step, SSM_PAIRS, PAIR_W, SSM_STATE), lambda b, c: (b, 0, 0, 0))
    if has_state:
        in_specs += [pl.BlockSpec((step, TAIL, SSM_XBC), lambda b, c: (b, 0, 0)), state_spec]
        args += [buf, s0]
    for prm in params:
        in_specs.append(pl.BlockSpec(prm.shape, lambda b, c: (0, 0)))
        args.append(prm)
    return pl.pallas_call(
        functools.partial(_ssd_kernel, lc=lc, ng=ng, nbk=nbk, has_state=has_state),
        grid=(nseq // step, nchunk),
        in_specs=in_specs,
        out_specs=[pl.BlockSpec((step, 1, lc, SSM_WIDTH), lambda b, c: (b, c, 0, 0)), state_spec],
        out_shape=[jax.ShapeDtypeStruct((nseq, nchunk, lc, SSM_WIDTH), F32),
                   jax.ShapeDtypeStruct((nseq, SSM_PAIRS, PAIR_W, SSM_STATE), F32)],
        scratch_shapes=[pltpu.VMEM((step, TAIL + lc, SSM_XBC), F32)],
        compiler_params=_cparams(("arbitrary", "arbitrary")),
        name="ssd_mixer_state" if has_state else "ssd_mixer",
    )(*args)


def _xattn_kernel(q_ref, k_ref, v_ref, o_ref):
    for h in range(MEM_HEADS):
        sl = slice(h * MEM_HD, (h + 1) * MEM_HD)
        k = k_ref[0, :, sl]
        v = v_ref[0, :, sl]
        s = _bdot_nt(q_ref[0, :, sl], k) * (MEM_HD ** -0.5)
        s = s - jnp.max(s, axis=-1, keepdims=True)
        p = jnp.exp(s)
        p = p / jnp.sum(p, axis=-1, keepdims=True)
        o_ref[0, :, sl] = _bdot(p, v)


def _xattn_cache_kernel(q_ref, k_ref, v_ref, o_ref, *, tl):
    k2 = k_ref[...].reshape(MEM_LEN * MEM_HEADS, MEM_HD)
    v2 = v_ref[...].reshape(MEM_LEN * MEM_HEADS, MEM_HD)
    q4 = jnp.concatenate([q_ref[0, :, h * MEM_HD:(h + 1) * MEM_HD] for h in range(MEM_HEADS)], axis=0)
    s = _bdot_nt(q4, k2) * (MEM_HD ** -0.5)
    row_head = jnp.right_shift(lax.broadcasted_iota(jnp.int32, s.shape, 0), tl.bit_length() - 1)
    col_head = jnp.bitwise_and(lax.broadcasted_iota(jnp.int32, s.shape, 1), MEM_HEADS - 1)
    s = jnp.where(row_head == col_head, s, -jnp.inf)
    s = s - jnp.max(s, axis=-1, keepdims=True)
    p = jnp.exp(s)
    p = p / jnp.sum(p, axis=-1, keepdims=True)
    o4 = _bdot(p, v2)
    for h in range(MEM_HEADS):
        o_ref[0, :, h * MEM_HD:(h + 1) * MEM_HD] = o4[h * tl:(h + 1) * tl, :]


def xattn_cache(q3, row0, nseq, tl, cache_k, cache_v):
    assert tl & (tl - 1) == 0 and MEM_HEADS & (MEM_HEADS - 1) == 0
    kv_spec = pl.BlockSpec((None, None, MEM_LEN, MEM_HEADS, MEM_HD), lambda b: (0, b, 0, 0, 0))
    return pl.pallas_call(
        functools.partial(_xattn_cache_kernel, tl=tl),
        grid=(nseq,),
        in_specs=[pl.BlockSpec((1, tl, D_MODEL), lambda b: (row0 + b, 0, 0)), kv_spec, kv_spec],
        out_specs=pl.BlockSpec((1, tl, D_MODEL), lambda b: (b, 0, 0)),
        out_shape=jax.ShapeDtypeStruct((nseq, tl, D_MODEL), F32),
        compiler_params=_cparams(("arbitrary",)),
        name="xattn_cache",
    )(q3, cache_k, cache_v)


def xattn_core(q3, row0, nseq, ntile, tl, mem_k, mem_v, kv_col=(0, 0)):
    kv_specs = [pl.BlockSpec((1, MEM_LEN, D_MODEL), lambda b, t, c_=c_: (b, 0, c_)) for c_ in kv_col]
    return pl.pallas_call(
        _xattn_kernel,
        grid=(nseq, ntile),
        in_specs=[pl.BlockSpec((1, tl, D_MODEL), lambda b, t: (row0 + b * ntile + t, 0, 0))] + kv_specs,
        out_specs=pl.BlockSpec((1, tl, D_MODEL), lambda b, t: (b * ntile + t, 0, 0)),
        out_shape=jax.ShapeDtypeStruct((nseq * ntile, tl, D_MODEL), F32),
        compiler_params=_cparams(("arbitrary", "arbitrary")),
        name="xattn_core",
    )(q3, mem_k, mem_v)


def _top16_rows(s, break_ties):
    n = s.shape[0]
    rank = jnp.full(s.shape, NOT_TOP, F32)
    vals = []
    v = s
    for a in range(PEER_TOPK):
        m = jnp.max(v, axis=0, keepdims=True)
        hit = v == m
        if break_ties:
            iota = lax.broadcasted_iota(jnp.int32, s.shape, 0).astype(F32)
            hit = iota == jnp.min(jnp.where(hit, iota, float(n)), axis=0, keepdims=True)
        rank = jnp.where(hit, float(a), rank)
        v = jnp.where(hit, -jnp.inf, v)
        vals.append(m)
    return jnp.concatenate(vals, axis=0), rank


def _pair_top16(s0v, s1v):
    t = s0v.shape[1]
    iota = lax.broadcasted_iota(jnp.int32, (PEER_TOPK, t), 0).astype(F32)
    n = jnp.zeros((PEER_TOPK, t), F32)
    front = s0v + s1v[0:1, :]
    top = s0v[0:1, :] + s1v[0:1, :]
    z = jnp.zeros((1, t), F32)
    for _ in range(PEER_TOPK):
        m = jnp.max(front, axis=0, keepdims=True)
        a_star = jnp.min(jnp.where(front == m, iota, float(PEER_TOPK)), axis=0, keepdims=True)
        hit = iota == a_star
        z = z + jnp.exp(m - top)
        n = jnp.where(hit, n + 1.0, n)
        nxt = jnp.full((PEER_TOPK, t), -jnp.inf, F32)
        for b in range(1, PEER_TOPK):
            nxt = jnp.where(n == float(b), s1v[b:b + 1, :], nxt)
        front = jnp.where(hit, s0v + nxt, front)
    return n, z


def _peer_route_kernel(x_ref, g_ref, wq_ref, sk_ref, xn_ref, e0_ref, ni_ref, e1_ref, rj_ref, q_ref):
    xn = _rmsnorm_rows(x_ref[...], g_ref[...]).astype(BF16)
    xn_ref[...] = xn
    q_ref[...] = lax.dot_general(wq_ref[...], xn, (((1,), (1,)), ((), ())),
                                 preferred_element_type=F32).astype(BF16)
    tt = x_ref.shape[0]

    def route(break_ties):
        bad = jnp.zeros((1, LANES), F32)
        for lb in range(tt // LANES):
            ls = slice(lb * LANES, (lb + 1) * LANES)
            for h in range(PEER_HEADS):
                halves = []
                for c in range(2):
                    r0 = (2 * h + c) * PEER_HALF
                    s = jnp.dot(sk_ref[c], q_ref[r0:r0 + PEER_HALF, ls],
                                preferred_element_type=F32)
                    vals, rank = _top16_rows(s, break_ties)
                    halves.append((s, vals, rank))
                (s0, s0v, rank0), (s1, s1v, rank1) = halves
                n, z = _pair_top16(s0v, s1v)
                in0 = rank0 < float(PEER_TOPK)
                in1 = rank1 < float(PEER_TOPK)
                if not break_ties:
                    for inside in (in0, in1):
                        count = jnp.sum(jnp.where(inside, 1.0, 0.0), axis=0, keepdims=True)
                        bad = jnp.maximum(bad, jnp.abs(count - float(PEER_TOPK)))
                e0 = jnp.where(in0, jnp.exp(jnp.where(in0, s0 - s0v[0:1, :], 0.0)), 0.0) / z
                e1 = jnp.where(in1, jnp.exp(jnp.where(in1, s1 - s1v[0:1, :], 0.0)), 0.0)
                ni = jnp.zeros_like(rank0)
                for a in range(PEER_TOPK):
                    ni = jnp.where(rank0 == float(a), n[a:a + 1, :], ni)
                e0_ref[h, :, ls] = e0
                ni_ref[h, :, ls] = ni
                e1_ref[h, :, ls] = e1.astype(BF16)
                rj_ref[h, :, ls] = rank1.astype(BF16)
        return bad

    has_ties = jnp.max(route(break_ties=False)) > 0.0

    @pl.when(has_ties)
    def _():
        route(break_ties=True)


def peer_route(x, g, wq_t, sub_keys, *, tt):
    n = x.shape[0]
    fspec = pl.BlockSpec((PEER_HEADS, PEER_NKEYS, tt), lambda i: (0, 0, i))

    def fac(dtype):
        return jax.ShapeDtypeStruct((PEER_HEADS, PEER_NKEYS, n), dtype)

    return pl.pallas_call(
        _peer_route_kernel,
        grid=(n // tt,),
        in_specs=[pl.BlockSpec((tt, D_MODEL), lambda i: (i, 0)),
                  pl.BlockSpec((1, D_MODEL), lambda i: (0, 0)),
                  pl.BlockSpec(wq_t.shape, lambda i: (0, 0)),
                  pl.BlockSpec(sub_keys.shape, lambda i: (0, 0, 0))],
        out_specs=[pl.BlockSpec((tt, D_MODEL), lambda i: (i, 0)), fspec, fspec, fspec, fspec],
        out_shape=[jax.ShapeDtypeStruct((n, D_MODEL), BF16), fac(F32), fac(F32), fac(BF16), fac(BF16)],
        scratch_shapes=[pltpu.VMEM((PEER_HEADS * 2 * PEER_HALF, tt), BF16)],
        compiler_params=_cparams(("arbitrary",)),
        name="peer_route",
    )(x, g.reshape(1, D_MODEL), wq_t, sub_keys)


def _peer_dense_kernel(xn_ref, u_ref, vt_ref, vtp_ref, e0_ref, ni_ref, e1_ref, rj_ref, x_ref, gf_ref,
                       yp_ref, ys_ref, acc_ref, a_ref, p_ref, *, te, np_tiles):
    i_tok = pl.program_id(0)
    e = pl.program_id(1)

    @pl.when(e == 0)
    def _():
        acc_ref[...] = jnp.zeros_like(acc_ref)
        p_ref[...] = jnp.zeros_like(p_ref)

    zero = jnp.zeros((), BF16)
    rows_per_sub = PEER_SUB // PEER_NKEYS
    nsub = te // PEER_SUB
    tt = xn_ref.shape[0]
    halves = (slice(0, tt // 2), slice(tt // 2, tt))
    last = slice((nsub - 1) * PEER_SUB, nsub * PEER_SUB)

    def result_row(ref, r0, hs):
        return ref[r0:r0 + 1, hs.start:hs.start + LANES]

    def pre_activations(s, hs):
        rs = slice(s * PEER_SUB, (s + 1) * PEER_SUB)
        a_ref[rs, hs] = lax.dot_general(u_ref[rs, :], xn_ref[hs, :], (((1,), (1,)), ((), ())),
                                        preferred_element_type=F32)
        return result_row(a_ref, rs.start, hs)

    def expert_out(v_ref, rs, hs):
        acc_ref[:, hs] += jnp.dot(v_ref, p_ref[rs, hs], preferred_element_type=F32)
        return result_row(acc_ref, 0, hs)

    def routing_weights(r, after):
        def all_rows(row):
            tile = jnp.broadcast_to(row, (BF16_SUBLANES, tt)).astype(BF16)
            return jnp.concatenate([tile] * (PEER_NKEYS // BF16_SUBLANES), axis=0)

        w = None
        for h in range(PEER_HEADS):
            e0f = e0_ref[h, r:r + 1, :]
            if after is not None:
                e0f = jnp.where(e >= 0, e0f, jnp.concatenate([after] * (tt // LANES), axis=1))
            wh = jnp.where(rj_ref[h] < all_rows(ni_ref[h, r:r + 1, :]), all_rows(e0f) * e1_ref[h], zero)
            w = wh if w is None else w + wh
        p_ref[r * PEER_NKEYS:(r + 1) * PEER_NKEYS, :] = w

    def weighted_activations(r):
        rr = slice(r * PEER_NKEYS, (r + 1) * PEER_NKEYS)
        a = a_ref[rr, :].astype(BF16)
        act = 0.5 * a * (1.0 + lax.erf(a * (2.0 ** -0.5)))
        p_ref[rr, :] = p_ref[rr, :] * act

    assert nsub == 2 and rows_per_sub == 8
    first = slice(0, PEER_SUB)
    done = expert_out(vtp_ref[...], last, halves[0])
    for r in (0, 1, 2):
        routing_weights(r, None)
    done, prev = expert_out(vtp_ref[...], last, halves[1]), done
    for r in (3, 4, 5):
        routing_weights(r, prev)
    done, prev = pre_activations(0, halves[0]), done
    for r in (6, 7, 8):
        routing_weights(r, prev)
    done, prev = pre_activations(0, halves[1]), done
    for r in (9, 10, 11):
        routing_weights(r, prev)
    done, prev = pre_activations(1, halves[0]), done
    for r in (0, 1, 2, 3):
        weighted_activations(r)
    done, prev = pre_activations(1, halves[1]), done
    for r in (4, 5, 6, 7):
        weighted_activations(r)
    done, prev = expert_out(vt_ref[:, first], first, halves[0]), done
    for r in (12, 13, 14):
        routing_weights(r, prev)
    done, prev = expert_out(vt_ref[:, first], first, halves[1]), done
    routing_weights(15, prev)
    for r in range(8, 16):
        weighted_activations(r)

    @pl.when(e == pl.num_programs(1) - 1)
    def _():
        acc_ref[...] += jnp.dot(vt_ref[:, last], p_ref[last, :], preferred_element_type=F32)
        y = _rmsnorm_rows(x_ref[...] + jnp.transpose(acc_ref[...]), gf_ref[...])

        @pl.when(i_tok < np_tiles)
        def _():
            yp_ref[...] = y

        @pl.when(i_tok >= np_tiles)
        def _():
            ys_ref[...] = y


def peer_dense(xn, u, v_t, e0, ni, e1, rj, x, g_final, n_p, *, tt, te):
    n = xn.shape[0]
    np_tiles = n_p // tt
    fspec = pl.BlockSpec((PEER_HEADS, PEER_NKEYS, tt), lambda i, e: (0, 0, i))
    rspec = pl.BlockSpec((PEER_HEADS, te // PEER_NKEYS, tt), lambda i, e: (0, e, i))
    return pl.pallas_call(
        functools.partial(_peer_dense_kernel, te=te, np_tiles=np_tiles),
        grid=(n // tt, PEER_EXPERTS // te),
        in_specs=[pl.BlockSpec((tt, D_MODEL), lambda i, e: (i, 0)),
                  pl.BlockSpec((te, D_MODEL), lambda i, e: (e, 0)),
                  pl.BlockSpec((D_MODEL, te), lambda i, e: (0, e)),
                  pl.BlockSpec((D_MODEL, PEER_SUB), lambda i, e: (0, jnp.maximum(e * (te // PEER_SUB) - 1, 0))),
                  rspec, rspec, fspec, fspec,
                  pl.BlockSpec((tt, D_MODEL), lambda i, e: (i, 0)),
                  pl.BlockSpec((1, D_MODEL), lambda i, e: (0, 0))],
        out_specs=[pl.BlockSpec((tt, D_MODEL), lambda i, e: (jnp.minimum(i, np_tiles - 1), 0)),
                   pl.BlockSpec((tt, D_MODEL), lambda i, e: (jnp.maximum(i - np_tiles, 0), 0))],
        out_shape=[jax.ShapeDtypeStruct((n_p, D_MODEL), F32),
                   jax.ShapeDtypeStruct((n - n_p, D_MODEL), F32)],
        scratch_shapes=[pltpu.VMEM((D_MODEL, tt), F32),
                        pltpu.VMEM((te, tt), F32),
                        pltpu.VMEM((te, tt), BF16)],
        compiler_params=_cparams(("arbitrary", "arbitrary")),
        name="peer_dense",
    )(xn, u, v_t, v_t, e0, ni, e1, rj, x, g_final.reshape(1, D_MODEL))


def _lane_row(vec, start, width=GATE_W):
    return jnp.zeros((1, width), F32).at[0, start:start + vec.shape[0]].set(vec.astype(F32))


def _rep_matrix(first_row, heads, lanes_per_head, rows=GATE_W):
    r = jnp.arange(rows)[:, None]
    c = jnp.arange(heads * lanes_per_head)[None, :]
    return ((c // lanes_per_head) + first_row == r).astype(BF16)


def _pad_tail(buf):
    return jnp.pad(buf, ((0, 0), (TAIL - (CONV_K - 1), 0), (0, 0)))


def kernel(x_prompt, x_sample, state_dn_conv, state_dn, state_ssm_conv, state_ssm, cache_mem_k, cache_mem_v,
           mem_prompt, g_mix, w_in, dn_conv_w, dn_A_log, dn_dt_bias, dn_norm_w, ssm_conv_w, ssm_conv_b,
           ssm_A_log, ssm_dt_bias, ssm_D, ssm_norm_w, w_out, g_xattn, g_mem, w_xq, w_mkv, w_xo, g_ffn, w_pq,
           peer_sub_keys, peer_u, peer_v, g_final):
    depth = g_mix.shape[0]
    assert depth == 1
    bp, lp, d = x_prompt.shape
    bs, ls, _ = x_sample.shape
    n_p, n_s = bp * lp, bs * ls
    n = n_p + n_s
    ncp = lp // CHUNK
    tm = 512
    x = (x_prompt.reshape(n_p, d), x_sample.reshape(n_s, d))

    wi = w_in[0]
    o_xbc = DN_QKV
    o_zdn = o_xbc + SSM_XBC
    o_zssm = o_zdn + DN_WIDTH
    o_small = o_zssm + SSM_WIDTH
    n_small = 2 * DN_HEADS + SSM_HEADS
    w_cat = jnp.concatenate([
        wi[:, :DN_QKV], wi[:, o_xbc:o_xbc + SSM_XBC],
        wi[:, o_small:o_small + n_small], jnp.zeros((d, COL_ZDN - COL_GATE - n_small), F32),
        wi[:, o_zdn:o_zdn + DN_WIDTH], wi[:, o_zssm:o_zssm + SSM_WIDTH]], axis=1).astype(BF16)

    proj = norm_matmul(x, g_mix[0], w_cat, tm=2 * tm, tn=1024)
    proj_c = proj.reshape(n // CHUNK, CHUNK, PROJ_COLS)
    proj_s = proj.reshape(n // ls, ls, PROJ_COLS)

    gdn_params = (dn_conv_w[0],
                  jnp.concatenate([_lane_row(dn_A_log[0], DN_HEADS), _lane_row(dn_dt_bias[0], DN_HEADS)], axis=0),
                  dn_norm_w[0].reshape(1, DN_DK),
                  _rep_matrix(0, DN_HEADS, DN_DK), _rep_matrix(DN_HEADS, DN_HEADS, DN_DK))
    ssd_params = (ssm_conv_w[0], ssm_conv_b[0].reshape(1, SSM_XBC),
                  jnp.concatenate([_lane_row(ssm_A_log[0], 2 * DN_HEADS), _lane_row(ssm_dt_bias[0], 2 * DN_HEADS)],
                                  axis=0),
                  jnp.repeat(ssm_D[0], SSM_HEADDIM).reshape(1, SSM_WIDTH),
                  ssm_norm_w[0].reshape(1, SSM_WIDTH),
                  _rep_matrix(2 * DN_HEADS, SSM_HEADS, SSM_HEADDIM), _rep_matrix(2 * DN_HEADS, SSM_HEADS, LANES))

    odn_p, p_dn = gdn_mixer(proj_c, 0, bp, ncp, CHUNK, 4, 1, gdn_params)
    odn_s, s_dn = gdn_mixer(proj_s, n_p // ls, bs, 1, ls, 1, 8, gdn_params,
                            buf=_pad_tail(state_dn_conv[0]), s0=state_dn[0])
    ossm_p, p_ss = ssd_mixer(proj_c, 0, bp, ncp, CHUNK, 4, 1, ssd_params)
    ossm_s, s_ss = ssd_mixer(proj_s, n_p // ls, bs, 1, ls, 1, 8, ssd_params,
                             buf=_pad_tail(state_ssm_conv[0]),
                             s0=state_ssm[0].reshape(bs, SSM_PAIRS, PAIR_W, SSM_STATE))
    wo = w_out[0].astype(BF16)
    x1 = mixer_out((odn_p.reshape(n_p, DN_WIDTH), odn_s.reshape(n_s, DN_WIDTH)),
                   (ossm_p.reshape(n_p, SSM_WIDTH), ossm_s.reshape(n_s, SSM_WIDTH)),
                   wo[:DN_WIDTH], wo[DN_WIDTH:], x, tm=tm)

    last_c = slice(ncp - 1, bp * ncp, ncp)
    hist_c = slice(CHUNK - (CONV_K - 1), CHUNK)
    hist_s = slice(ls - (CONV_K - 1), ls)
    p_dnc = proj_c[last_c, hist_c, COL_QKV:COL_QKV + DN_QKV]
    p_sc = proj_c[last_c, hist_c, COL_XBC:COL_XBC + SSM_XBC]
    s_dnc = proj_s[n_p // ls:, hist_s, COL_QKV:COL_QKV + DN_QKV]
    s_sc = proj_s[n_p // ls:, hist_s, COL_XBC:COL_XBC + SSM_XBC]

    mkv = norm_matmul(mem_prompt.reshape(bp * MEM_LEN, d), g_mem[0], w_mkv[0].astype(BF16), tm=tm, tn=1024)
    mkv3 = mkv.reshape(bp, MEM_LEN, 2 * d)
    q = norm_matmul(x1, g_xattn[0], w_xq[0].astype(BF16), tm=tm, tn=1024)
    tl = 512
    a_p = xattn_core(q.reshape(n // tl, tl, d), 0, bp, lp // tl, tl, mkv3, mkv3, kv_col=(0, 1))
    a_s = xattn_cache(q.reshape(n // ls, ls, d), n_p // ls, bs, ls, cache_mem_k, cache_mem_v)
    x2 = matmul_res((a_p.reshape(n_p, d), a_s.reshape(n_s, d)), w_xo[0].astype(BF16), x1, tm=tm)

    xn, e0, ni, e1, rj = peer_route(x2, g_ffn[0], jnp.transpose(w_pq[0]).astype(BF16),
                                    peer_sub_keys[0].astype(BF16), tt=256)
    y_p, y_s = peer_dense(xn, peer_u[0].astype(BF16), jnp.transpose(peer_v[0]).astype(BF16),
                          e0, ni, e1, rj, x2, g_final, n_p, tt=512, te=2048)

    def heads(m):
        return m.reshape(bp, MEM_LEN, MEM_HEADS, MEM_HD)[None]

    return (y_p.reshape(bp, lp, d), y_s.reshape(bs, ls, d),
            p_dnc[None], p_dn[None], p_sc[None], p_ss.reshape(bp, SSM_HEADS, SSM_HEADDIM, SSM_STATE)[None],
            heads(mkv3[:, :, :d]), heads(mkv3[:, :, d:]),
            s_dnc[None], s_dn[None], s_sc[None], s_ss.reshape(bs, SSM_HEADS, SSM_HEADDIM, SSM_STATE)[None])
```

```python
import functools

import jax
import jax.numpy as jnp
from jax import lax
from jax.experimental import pallas as pl
from jax.experimental.pallas import tpu as pltpu

F32 = jnp.float32
BF16 = jnp.bfloat16
EPS = 1e-6

D_MODEL = 1024
CONV_K = 4
CHUNK = 64
DN_HEADS = 8
DN_DK = 128
DN_WIDTH = 1024
DN_QKV = 3072
SSM_HEADS = 16
SSM_HEADDIM = 64
SSM_STATE = 128
SSM_WIDTH = 1024
SSM_XBC = 1536
MEM_LEN = 256
MEM_HEADS = 4
MEM_HD = 256
PEER_HEADS = 8
PEER_NKEYS = 128
PEER_EXPERTS = PEER_NKEYS * PEER_NKEYS
PEER_TOPK = 16
PEER_HALF = 128

COL_QKV = 0
COL_XBC = 3072
COL_GATE = 4608
COL_ZDN = 5120
COL_ZSSM = 6144
PROJ_COLS = 7168
GATE_W = 128
TAIL = 8
INV_BLOCK = 16
NOT_TOP = 99.0
LANES = 128
BF16_SUBLANES = 16
PEER_SUB = 1024

VMEM_LIMIT = 56 * 1024 * 1024


def _cparams(sem):
    return pltpu.CompilerParams(dimension_semantics=sem, vmem_limit_bytes=VMEM_LIMIT)


def _bdot(a, b):
    return jnp.dot(a.astype(BF16), b.astype(BF16), preferred_element_type=F32)


def _bdot_nt(a, b):
    return lax.dot_general(a.astype(BF16), b.astype(BF16), (((1,), (1,)), ((), ())),
                           preferred_element_type=F32)


def _bdot_tn(a, b):
    return lax.dot_general(a.astype(BF16), b.astype(BF16), (((0,), (0,)), ((), ())),
                           preferred_element_type=F32)


def _split3(x):
    hi = x.astype(BF16)
    r = x - hi.astype(F32)
    mid = r.astype(BF16)
    lo = (r - mid.astype(F32)).astype(BF16)
    return hi, mid, lo


def _dot_sel_rhs(x, sel):
    hi, mid, lo = _split3(x)
    d = functools.partial(jnp.dot, preferred_element_type=F32)
    return d(hi, sel) + d(mid, sel) + d(lo, sel)


def _dot_sel_lhs(sel, x):
    hi, mid, lo = _split3(x)
    d = functools.partial(jnp.dot, preferred_element_type=F32)
    return d(sel, hi) + d(sel, mid) + d(sel, lo)


def _silu(x):
    return x * jax.nn.sigmoid(x)


def _softplus(x):
    return jnp.maximum(x, 0.0) + jnp.log1p(jnp.exp(-jnp.abs(x)))


def _unit_lower_inverse(lms, n):
    row = lax.broadcasted_iota(jnp.int32, (n, n), 0)
    col = lax.broadcasted_iota(jnp.int32, (n, n), 1)
    eye = (row == col).astype(F32)

    def nilpotent_inverse(xs, index):
        invs = [eye - x for x in xs]
        ps = xs
        k = 2
        while k < index:
            ps = [_bdot(p, p) for p in ps]
            invs = [inv + _bdot(inv, p) for inv, p in zip(invs, ps)]
            k *= 2
        return invs

    if n <= INV_BLOCK:
        return nilpotent_inverse(lms, n)
    shift = INV_BLOCK.bit_length() - 1
    same = jnp.right_shift(row, shift) == jnp.right_shift(col, shift)
    dinvs = nilpotent_inverse([jnp.where(same, lm, 0.0) for lm in lms], INV_BLOCK)
    fs = [_bdot(dinv, jnp.where(same, 0.0, lm)) for dinv, lm in zip(dinvs, lms)]
    finvs = nilpotent_inverse(fs, n // INV_BLOCK)
    return [_bdot(finv, dinv) for finv, dinv in zip(finvs, dinvs)]


def _causal_conv(x, xp_ref, w, lc):
    xp_ref[TAIL:TAIL + lc, :] = x
    y = x * w[3:4, :]
    for s in range(1, CONV_K):
        y = y + xp_ref[TAIL - s:TAIL - s + lc, :] * w[CONV_K - 1 - s:CONV_K - s, :]
    xp_ref[0:TAIL, :] = xp_ref[lc:lc + TAIL, :]
    return y


def _decay_matrix(col, row, lower_incl):
    diff = jnp.where(lower_incl, col - row, 0.0)
    return jnp.where(lower_incl, jnp.exp(diff), 0.0)


def _pair_specs(tm, k, np_tiles, two_axes):
    if two_axes:
        return [pl.BlockSpec((tm, k), lambda i, j: (jnp.minimum(i, np_tiles - 1), 0)),
                pl.BlockSpec((tm, k), lambda i, j: (jnp.maximum(i - np_tiles, 0), 0))]
    return [pl.BlockSpec((tm, k), lambda i: (jnp.minimum(i, np_tiles - 1), 0)),
            pl.BlockSpec((tm, k), lambda i: (jnp.maximum(i - np_tiles, 0), 0))]


def _pick(is_prompt, p_ref, s_ref):
    return jnp.where(is_prompt, p_ref[...], s_ref[...])


def _rmsnorm_rows(x, g):
    ms = jnp.mean(x * x, axis=-1, keepdims=True)
    return x * lax.rsqrt(ms + EPS) * g


def _norm_matmul_kernel(*refs, np_tiles):
    if np_tiles is None:
        x_ref, g_ref, w_ref, o_ref, xn_ref = refs
    else:
        xp_ref, xs_ref, g_ref, w_ref, o_ref, xn_ref = refs
        is_prompt = pl.program_id(0) < np_tiles

    @pl.when(pl.program_id(1) == 0)
    def _():
        x = x_ref[...] if np_tiles is None else _pick(is_prompt, xp_ref, xs_ref)
        xn_ref[...] = _rmsnorm_rows(x, g_ref[...]).astype(BF16)

    o_ref[...] = jnp.dot(xn_ref[...], w_ref[...], preferred_element_type=F32).astype(o_ref.dtype)


def norm_matmul(x, g, w, *, tm, tn, out_dtype=F32):
    pair = isinstance(x, tuple)
    n = sum(a.shape[0] for a in x) if pair else x.shape[0]
    k, m = w.shape
    assert n % tm == 0 and m % tn == 0
    if pair:
        assert x[0].shape[0] % tm == 0
        np_tiles = x[0].shape[0] // tm
        x_specs = _pair_specs(tm, k, np_tiles, True)
        x_args = list(x)
    else:
        np_tiles = None
        x_specs = [pl.BlockSpec((tm, k), lambda i, j: (i, 0))]
        x_args = [x]
    return pl.pallas_call(
        functools.partial(_norm_matmul_kernel, np_tiles=np_tiles),
        grid=(n // tm, m // tn),
        in_specs=x_specs + [pl.BlockSpec((1, k), lambda i, j: (0, 0)),
                            pl.BlockSpec((k, tn), lambda i, j: (0, j))],
        out_specs=pl.BlockSpec((tm, tn), lambda i, j: (i, j)),
        out_shape=jax.ShapeDtypeStruct((n, m), out_dtype),
        scratch_shapes=[pltpu.VMEM((tm, k), BF16)],
        compiler_params=_cparams(("arbitrary", "arbitrary")),
        name="norm_matmul",
    )(*x_args, g.reshape(1, k), w)


def _mixer_out_kernel(dp_ref, ds_ref, sp_ref, ss_ref, w1_ref, w2_ref, xp_ref, xs_ref, o_ref, *, np_tiles):
    is_prompt = pl.program_id(0) < np_tiles
    a1 = _pick(is_prompt, dp_ref, ds_ref).astype(BF16)
    a2 = _pick(is_prompt, sp_ref, ss_ref).astype(BF16)
    acc = jnp.dot(a1, w1_ref[...], preferred_element_type=F32)
    acc = acc + jnp.dot(a2, w2_ref[...], preferred_element_type=F32)
    o_ref[...] = _pick(is_prompt, xp_ref, xs_ref) + acc


def mixer_out(o_dn, o_ssm, w1, w2, x, *, tm):
    n = x[0].shape[0] + x[1].shape[0]
    k = w1.shape[0]
    m = w1.shape[1]
    np_tiles = x[0].shape[0] // tm
    wspec = pl.BlockSpec((k, m), lambda i: (0, 0))
    return pl.pallas_call(
        functools.partial(_mixer_out_kernel, np_tiles=np_tiles),
        grid=(n // tm,),
        in_specs=(_pair_specs(tm, k, np_tiles, False) + _pair_specs(tm, k, np_tiles, False)
                  + [wspec, wspec] + _pair_specs(tm, m, np_tiles, False)),
        out_specs=pl.BlockSpec((tm, m), lambda i: (i, 0)),
        out_shape=jax.ShapeDtypeStruct((n, m), F32),
        compiler_params=_cparams(("arbitrary",)),
        name="mixer_out",
    )(*o_dn, *o_ssm, w1, w2, *x)


def _matmul_res_kernel(ap_ref, as_ref, w_ref, r_ref, o_ref, *, np_tiles):
    a = _pick(pl.program_id(0) < np_tiles, ap_ref, as_ref).astype(BF16)
    o_ref[...] = r_ref[...] + jnp.dot(a, w_ref[...], preferred_element_type=F32)


def matmul_res(a, w, res, *, tm):
    n, m = res.shape
    k = w.shape[0]
    np_tiles = a[0].shape[0] // tm
    return pl.pallas_call(
        functools.partial(_matmul_res_kernel, np_tiles=np_tiles),
        grid=(n // tm,),
        in_specs=_pair_specs(tm, k, np_tiles, False) + [pl.BlockSpec((k, m), lambda i: (0, 0)),
                                                        pl.BlockSpec((tm, m), lambda i: (i, 0))],
        out_specs=pl.BlockSpec((tm, m), lambda i: (i, 0)),
        out_shape=jax.ShapeDtypeStruct((n, m), F32),
        compiler_params=_cparams(("arbitrary",)),
        name="matmul_res",
    )(*a, w, res)


def _gdn_kernel(*refs, lc, ng, nbk, has_state):
    refs = list(refs)

    def take(k):
        out = refs[:k]
        del refs[:k]
        return out

    qkv_refs, gate_refs, z_refs = take(ng), take(ng), take(ng)
    if has_state:
        buf_ref, s0_ref = take(2)
    cw_ref, gp_ref, nw_ref, repb_ref, repg_ref, o_ref, s_ref, xp_ref = take(8)
    nseq = ng * nbk
    where = [divmod(j, nbk) for j in range(nseq)]

    @pl.when(pl.program_id(1) == 0)
    def _():
        if has_state:
            xp_ref[:, 0:TAIL, :] = buf_ref[...]
            s_ref[...] = s0_ref[...]
        else:
            xp_ref[:, 0:TAIL, :] = jnp.zeros((nseq, TAIL, DN_QKV), F32)
            s_ref[...] = jnp.zeros(s_ref.shape, F32)

    row = lax.broadcasted_iota(jnp.int32, (lc, lc), 0)
    col = lax.broadcasted_iota(jnp.int32, (lc, lc), 1)
    incl = row >= col
    strict = row > col
    tri = incl.astype(BF16)

    seqs = []
    for j, (g_, k_) in enumerate(where):
        qkv = _silu(_causal_conv(qkv_refs[g_][k_], xp_ref.at[j], cw_ref[...], lc))
        gate = gate_refs[g_][k_]
        beta_all = jax.nn.sigmoid(gate)
        g_all = -jnp.exp(gp_ref[0:1, :]) * _softplus(gate + gp_ref[1:2, :])
        gc = _dot_sel_lhs(tri, g_all)
        gc_b = _dot_sel_rhs(gc, repg_ref[...])
        gc_last_b = gc_b[lc - 1:lc, :]
        seqs.append(dict(qkv=qkv, gc_t=jnp.transpose(gc), gc_b=gc_b,
                         beta_b=_dot_sel_rhs(beta_all, repb_ref[...]),
                         egc_b=jnp.exp(gc_b), ekd_b=jnp.exp(gc_last_b - gc_b), egt_b=jnp.exp(gc_last_b)))

    units = [(j, h) for j in range(nseq) for h in range(DN_HEADS)]
    qs, ks, vbs, kbegs, decays = [], [], [], [], []
    for j, h in units:
        sq = seqs[j]
        sl = slice(h * DN_DK, (h + 1) * DN_DK)
        qh = sq["qkv"][:, h * DN_DK:(h + 1) * DN_DK]
        kh = sq["qkv"][:, DN_WIDTH + h * DN_DK:DN_WIDTH + (h + 1) * DN_DK]
        vh = sq["qkv"][:, 2 * DN_WIDTH + h * DN_DK:2 * DN_WIDTH + (h + 1) * DN_DK]
        qh = qh * lax.rsqrt(jnp.sum(qh * qh, axis=-1, keepdims=True) + EPS) * (DN_DK ** -0.5)
        kh = kh * lax.rsqrt(jnp.sum(kh * kh, axis=-1, keepdims=True) + EPS)
        bh = sq["beta_b"][:, sl]
        qs.append(qh)
        ks.append(kh)
        vbs.append(vh * bh)
        kbegs.append((kh * bh, sq["egc_b"][:, sl]))
        decays.append(_decay_matrix(sq["gc_b"][:, h * DN_DK:h * DN_DK + lc],
                                    sq["gc_t"][DN_HEADS + h:DN_HEADS + h + 1, :], incl))
    lowers = [jnp.where(strict, _bdot_nt(kb, kh) * dec, 0.0)
              for (kb, _), kh, dec in zip(kbegs, ks, decays)]
    attns = [_bdot_nt(qh, kh) * dec for qh, kh, dec in zip(qs, ks, decays)]
    tinvs = _unit_lower_inverse(lowers, lc)
    us = [_bdot(t, vb) for t, vb in zip(tinvs, vbs)]
    ws = [_bdot(t, kb * eg) for t, (kb, eg) in zip(tinvs, kbegs)]
    states = [s_ref[j, h] for j, h in units]
    v_news = [u - _bdot(w, s) for u, w, s in zip(us, ws, states)]
    os_ = [_bdot(qh * eg, s) + _bdot(a, vn)
           for qh, (_, eg), s, a, vn in zip(qs, kbegs, states, attns, v_news)]
    for (j, h), kh, s, vn, o in zip(units, ks, states, v_news, os_):
        g_, k_ = where[j]
        sl = slice(h * DN_DK, (h + 1) * DN_DK)
        sq = seqs[j]
        s_ref[j, h] = s * sq["egt_b"][:, sl] + _bdot_tn(kh * sq["ekd_b"][:, sl], vn)
        o = o * lax.rsqrt(jnp.mean(o * o, axis=-1, keepdims=True) + EPS) * nw_ref[...]
        o_ref[j, 0, :, sl] = o * _silu(z_refs[g_][k_, :, sl])


def gdn_mixer(proj3, row0, nseq, nchunk, lc, ng, nbk, params, buf=None, s0=None):
    has_state = buf is not None
    step = ng * nbk
    assert nseq % step == 0 and row0 % nbk == 0 and (nbk == 1 or nchunk == 1)

    def proj_specs(width, col_block):
        return [pl.BlockSpec((nbk, lc, width),
                             lambda b, c, g_=g_: (row0 // nbk + (b * ng + g_) * nchunk + c, 0, col_block))
                for g_ in range(ng)]

    in_specs = (proj_specs(DN_QKV, COL_QKV // DN_QKV) + proj_specs(GATE_W, COL_GATE // GATE_W)
                + proj_specs(DN_WIDTH, COL_ZDN // DN_WIDTH))
    args = [proj3] * (3 * ng)
    state_spec = pl.BlockSpec((step, DN_HEADS, DN_DK, DN_DK), lambda b, c: (b, 0, 0, 0))
    if has_state:
        in_specs += [pl.BlockSpec((step, TAIL, DN_QKV), lambda b, c: (b, 0, 0)), state_spec]
        args += [buf, s0]
    for prm in params:
        in_specs.append(pl.BlockSpec(prm.shape, lambda b, c: (0, 0)))
        args.append(prm)
    return pl.pallas_call(
        functools.partial(_gdn_kernel, lc=lc, ng=ng, nbk=nbk, has_state=has_state),
        grid=(nseq // step, nchunk),
        in_specs=in_specs,
        out_specs=[pl.BlockSpec((step, 1, lc, DN_WIDTH), lambda b, c: (b, c, 0, 0)), state_spec],
        out_shape=[jax.ShapeDtypeStruct((nseq, nchunk, lc, DN_WIDTH), F32),
                   jax.ShapeDtypeStruct((nseq, DN_HEADS, DN_DK, DN_DK), F32)],
        scratch_shapes=[pltpu.VMEM((step, TAIL + lc, DN_QKV), F32)],
        compiler_params=_cparams(("arbitrary", "arbitrary")),
        name="gdn_mixer_state" if has_state else "gdn_mixer",
    )(*args)


SSM_PAIRS = SSM_HEADS // 2
PAIR_W = 2 * SSM_HEADDIM
GROUP_W = SSM_WIDTH // 2


def _ssd_kernel(*refs, lc, ng, nbk, has_state):
    refs = list(refs)

    def take(k):
        out = refs[:k]
        del refs[:k]
        return out

    xbc_refs, gate_refs, z_refs = take(ng), take(ng), take(ng)
    if has_state:
        buf_ref, s0_ref = take(2)
    (cw_ref, cb_ref, gp_ref, dd_ref, nw_ref, reps_ref, repw_ref, o_ref, s_ref, xp_ref) = take(10)
    nseq = ng * nbk
    where = [divmod(j, nbk) for j in range(nseq)]

    @pl.when(pl.program_id(1) == 0)
    def _():
        if has_state:
            xp_ref[:, 0:TAIL, :] = buf_ref[...]
            s_ref[...] = s0_ref[...]
        else:
            xp_ref[:, 0:TAIL, :] = jnp.zeros((nseq, TAIL, SSM_XBC), F32)
            s_ref[...] = jnp.zeros(s_ref.shape, F32)

    row = lax.broadcasted_iota(jnp.int32, (lc, lc), 0)
    col = lax.broadcasted_iota(jnp.int32, (lc, lc), 1)
    incl = row >= col
    tri = incl.astype(BF16)
    first_head = lax.broadcasted_iota(jnp.int32, (lc, PAIR_W), 1) < SSM_HEADDIM

    seqs = []
    for j, (g_, k_) in enumerate(where):
        xbc = _silu(_causal_conv(xbc_refs[g_][k_], xp_ref.at[j], cw_ref[...], lc) + cb_ref[...])
        xs = xbc[:, :SSM_WIDTH]
        dt = _softplus(gate_refs[g_][k_] + gp_ref[1:2, :])
        acs = _dot_sel_lhs(tri, dt * (-jnp.exp(gp_ref[0:1, :])))
        acs_b = _dot_sel_rhs(acs, reps_ref[...])
        acs_w = _dot_sel_rhs(acs, repw_ref[...])
        seqs.append(dict(xbc=xbc, xs=xs, acs_t=jnp.transpose(acs), acs_w=acs_w,
                         xd=xs * _dot_sel_rhs(dt, reps_ref[...]),
                         eacs_b=jnp.exp(acs_b), ends_b=jnp.exp(acs_b[lc - 1:lc, :] - acs_b),
                         ecd_w=jnp.exp(acs_w[lc - 1:lc, :])))

    def bmat(sq, g):
        return sq["xbc"][:, SSM_WIDTH + g * SSM_STATE:SSM_WIDTH + (g + 1) * SSM_STATE]

    def cmat(sq, g):
        o = SSM_WIDTH + 2 * SSM_STATE
        return sq["xbc"][:, o + g * SSM_STATE:o + (g + 1) * SSM_STATE]

    cbs = [[_bdot_nt(cmat(sq, g), bmat(sq, g)) for g in range(2)] for sq in seqs]
    units = [(j, p) for j in range(nseq) for p in range(SSM_PAIRS)]
    lmats = [[_decay_matrix(seqs[j]["acs_w"][:, hh * LANES:hh * LANES + lc],
                            seqs[j]["acs_t"][2 * DN_HEADS + hh:2 * DN_HEADS + hh + 1, :], incl)
              for hh in (2 * p, 2 * p + 1)] for j, p in units]
    prevs = [s_ref[j, p] for j, p in units]
    y_diag, y_off, sts = [], [], []
    for (j, p), lm, prev in zip(units, lmats, prevs):
        sq = seqs[j]
        g = p // (SSM_PAIRS // 2)
        psl = slice(p * PAIR_W, (p + 1) * PAIR_W)
        xd_p = sq["xd"][:, psl]
        y_diag.append([_bdot(cbs[j][g] * lm[0], xd_p), _bdot(cbs[j][g] * lm[1], xd_p)])
        y_off.append(_bdot_nt(cmat(sq, g), prev))
        sts.append(_bdot_tn(xd_p * sq["ends_b"][:, psl], bmat(sq, g)))
    ys = []
    ssq = [[None, None] for _ in range(nseq)]
    for (j, p), yd, yo, st, prev in zip(units, y_diag, y_off, sts, prevs):
        sq = seqs[j]
        g_, k_ = where[j]
        g = p // (SSM_PAIRS // 2)
        psl = slice(p * PAIR_W, (p + 1) * PAIR_W)
        cd = jnp.concatenate(
            [jnp.broadcast_to(sq["ecd_w"][:, (2 * p) * LANES:(2 * p + 1) * LANES], (SSM_HEADDIM, SSM_STATE)),
             jnp.broadcast_to(sq["ecd_w"][:, (2 * p + 1) * LANES:(2 * p + 2) * LANES], (SSM_HEADDIM, SSM_STATE))],
            axis=0)
        s_ref[j, p] = prev * cd + st
        y = jnp.where(first_head, yd[0], yd[1]) + yo * sq["eacs_b"][:, psl] + dd_ref[:, psl] * sq["xs"][:, psl]
        y = y * _silu(z_refs[g_][k_, :, psl])
        s2 = jnp.sum(y * y, axis=-1, keepdims=True)
        ssq[j][g] = s2 if ssq[j][g] is None else ssq[j][g] + s2
        ys.append(y)
    for (j, p), y in zip(units, ys):
        g = p // (SSM_PAIRS // 2)
        psl = slice(p * PAIR_W, (p + 1) * PAIR_W)
        o_ref[j, 0, :, psl] = y * lax.rsqrt(ssq[j][g] * (1.0 / GROUP_W) + EPS) * nw_ref[:, psl]


def ssd_mixer(proj3, row0, nseq, nchunk, lc, ng, nbk, params, buf=None, s0=None):
    has_state = buf is not None
    step = ng * nbk
    assert nseq % step == 0 and row0 % nbk == 0 and (nbk == 1 or nchunk == 1)

    def proj_specs(width, col_block):
        return [pl.BlockSpec((nbk, lc, width),
                             lambda b, c, g_=g_: (row0 // nbk + (b * ng + g_) * nchunk + c, 0, col_block))
                for g_ in range(ng)]

    in_specs = (proj_specs(SSM_XBC, COL_XBC // SSM_XBC) + proj_specs(GATE_W, COL_GATE // GATE_W)
                + proj_specs(SSM_WIDTH, COL_ZSSM // SSM_WIDTH))
    args = [proj3] * (3 * ng)
    state_spec = pl.BlockSpec((step, SSM_PAIRS, PAIR_W, SSM_STATE), lambda b, c: (b, 0, 0, 0))
    if has_state:
        in_specs += [pl.BlockSpec((step, TAIL, SSM_XBC), lambda b, c: (b, 0, 0)), state_spec]
        args += [buf, s0]
    for prm in params:
        in_specs.append(pl.BlockSpec(prm.shape, lambda b, c: (0, 0)))
        args.append(prm)
    return pl.pallas_call(
        functools.partial(_ssd_kernel, lc=lc, ng=ng, nbk=nbk, has_state=has_state),
        grid=(nseq // step, nchunk),
        in_specs=in_specs,
        out_specs=[pl.BlockSpec((step, 1, lc, SSM_WIDTH), lambda b, c: (b, c, 0, 0)), state_spec],
        out_shape=[jax.ShapeDtypeStruct((nseq, nchunk, lc, SSM_WIDTH), F32),
                   jax.ShapeDtypeStruct((nseq, SSM_PAIRS, PAIR_W, SSM_STATE), F32)],
        scratch_shapes=[pltpu.VMEM((step, TAIL + lc, SSM_XBC), F32)],
        compiler_params=_cparams(("arbitrary", "arbitrary")),
        name="ssd_mixer_state" if has_state else "ssd_mixer",
    )(*args)


def _xattn_kernel(q_ref, k_ref, v_ref, o_ref):
    for h in range(MEM_HEADS):
        sl = slice(h * MEM_HD, (h + 1) * MEM_HD)
        k = k_ref[0, :, sl]
        v = v_ref[0, :, sl]
        s = _bdot_nt(q_ref[0, :, sl], k) * (MEM_HD ** -0.5)
        s = s - jnp.max(s, axis=-1, keepdims=True)
        p = jnp.exp(s)
        p = p / jnp.sum(p, axis=-1, keepdims=True)
        o_ref[0, :, sl] = _bdot(p, v)


def _xattn_cache_kernel(q_ref, k_ref, v_ref, o_ref, *, tl, nb):
    shape = (MEM_HEADS * tl, MEM_LEN * MEM_HEADS)
    row_head = jnp.right_shift(lax.broadcasted_iota(jnp.int32, shape, 0), tl.bit_length() - 1)
    col_head = jnp.bitwise_and(lax.broadcasted_iota(jnp.int32, shape, 1), MEM_HEADS - 1)
    own_head = row_head == col_head
    for j in range(nb):
        k2 = k_ref[j].reshape(MEM_LEN * MEM_HEADS, MEM_HD)
        v2 = v_ref[j].reshape(MEM_LEN * MEM_HEADS, MEM_HD)
        q4 = jnp.concatenate([q_ref[j, :, h * MEM_HD:(h + 1) * MEM_HD] for h in range(MEM_HEADS)], axis=0)
        s = jnp.where(own_head, _bdot_nt(q4, k2) * (MEM_HD ** -0.5), -jnp.inf)
        s = s - jnp.max(s, axis=-1, keepdims=True)
        p = jnp.exp(s)
        p = p / jnp.sum(p, axis=-1, keepdims=True)
        o4 = _bdot(p, v2)
        for h in range(MEM_HEADS):
            o_ref[j, :, h * MEM_HD:(h + 1) * MEM_HD] = o4[h * tl:(h + 1) * tl, :]


def xattn_cache(q3, row0, nseq, tl, nb, cache_k, cache_v):
    assert tl & (tl - 1) == 0 and MEM_HEADS & (MEM_HEADS - 1) == 0 and nseq % nb == 0 and row0 % nb == 0
    kv_spec = pl.BlockSpec((None, nb, MEM_LEN, MEM_HEADS, MEM_HD), lambda b: (0, b, 0, 0, 0))
    return pl.pallas_call(
        functools.partial(_xattn_cache_kernel, tl=tl, nb=nb),
        grid=(nseq // nb,),
        in_specs=[pl.BlockSpec((nb, tl, D_MODEL), lambda b: (row0 // nb + b, 0, 0)), kv_spec, kv_spec],
        out_specs=pl.BlockSpec((nb, tl, D_MODEL), lambda b: (b, 0, 0)),
        out_shape=jax.ShapeDtypeStruct((nseq, tl, D_MODEL), F32),
        compiler_params=_cparams(("arbitrary",)),
        name="xattn_cache",
    )(q3, cache_k, cache_v)


def xattn_core(q3, row0, nseq, ntile, tl, mem_k, mem_v, kv_col=(0, 0)):
    kv_specs = [pl.BlockSpec((1, MEM_LEN, D_MODEL), lambda b, t, c_=c_: (b, 0, c_)) for c_ in kv_col]
    return pl.pallas_call(
        _xattn_kernel,
        grid=(nseq, ntile),
        in_specs=[pl.BlockSpec((1, tl, D_MODEL), lambda b, t: (row0 + b * ntile + t, 0, 0))] + kv_specs,
        out_specs=pl.BlockSpec((1, tl, D_MODEL), lambda b, t: (b * ntile + t, 0, 0)),
        out_shape=jax.ShapeDtypeStruct((nseq * ntile, tl, D_MODEL), F32),
        compiler_params=_cparams(("arbitrary", "arbitrary")),
        name="xattn_core",
    )(q3, mem_k, mem_v)


def _top16_rows(s, break_ties):
    n = s.shape[0]
    rank = jnp.full(s.shape, NOT_TOP, F32)
    vals = []
    v = s
    for a in range(PEER_TOPK):
        m = jnp.max(v, axis=0, keepdims=True)
        hit = v == m
        if break_ties:
            iota = lax.broadcasted_iota(jnp.int32, s.shape, 0).astype(F32)
            hit = iota == jnp.min(jnp.where(hit, iota, float(n)), axis=0, keepdims=True)
        rank = jnp.where(hit, float(a), rank)
        v = jnp.where(hit, -jnp.inf, v)
        vals.append(m)
    return jnp.concatenate(vals, axis=0), rank


def _pair_top16(s0v, s1v):
    t = s0v.shape[1]
    iota = lax.broadcasted_iota(jnp.int32, (PEER_TOPK, t), 0).astype(F32)
    n = jnp.zeros((PEER_TOPK, t), F32)
    front = s0v + s1v[0:1, :]
    top = s0v[0:1, :] + s1v[0:1, :]
    z = jnp.zeros((1, t), F32)
    for _ in range(PEER_TOPK):
        m = jnp.max(front, axis=0, keepdims=True)
        a_star = jnp.min(jnp.where(front == m, iota, float(PEER_TOPK)), axis=0, keepdims=True)
        hit = iota == a_star
        z = z + jnp.exp(m - top)
        n = jnp.where(hit, n + 1.0, n)
        nxt = jnp.full((PEER_TOPK, t), -jnp.inf, F32)
        for b in range(1, PEER_TOPK):
            nxt = jnp.where(n == float(b), s1v[b:b + 1, :], nxt)
        front = jnp.where(hit, s0v + nxt, front)
    return n, z


def _peer_route_kernel(x_ref, g_ref, wq_ref, sk_ref, xn_ref, e0_ref, ni_ref, e1_ref, rj_ref, q_ref):
    xn = _rmsnorm_rows(x_ref[...], g_ref[...]).astype(BF16)
    xn_ref[...] = xn
    q_ref[...] = lax.dot_general(wq_ref[...], xn, (((1,), (1,)), ((), ())),
                                 preferred_element_type=F32).astype(BF16)
    tt = x_ref.shape[0]

    def route(break_ties):
        bad = jnp.zeros((1, LANES), F32)
        for lb in range(tt // LANES):
            ls = slice(lb * LANES, (lb + 1) * LANES)
            for h in range(PEER_HEADS):
                def ranked_half(c):
                    r0 = (2 * h + c) * PEER_HALF
                    s = jnp.dot(sk_ref[c], q_ref[r0:r0 + PEER_HALF, ls],
                                preferred_element_type=F32)
                    vals, rank = _top16_rows(s, break_ties)
                    inside = rank < float(PEER_TOPK)
                    ex = jnp.where(inside, jnp.exp(jnp.where(inside, s - vals[0:1, :], 0.0)), 0.0)
                    count = jnp.sum(jnp.where(inside, 1.0, 0.0), axis=0, keepdims=True)
                    return vals, rank, ex, jnp.abs(count - float(PEER_TOPK))

                s1v, rank1, e1, off1 = ranked_half(1)
                e1_ref[h, :, ls] = e1.astype(BF16)
                rj_ref[h, :, ls] = rank1.astype(BF16)
                s0v, rank0, e0, off0 = ranked_half(0)
                if not break_ties:
                    bad = jnp.maximum(bad, jnp.maximum(off0, off1))
                n, z = _pair_top16(s0v, s1v)
                e0_ref[h, :, ls] = e0 / z
                ni = jnp.zeros_like(rank0)
                for a in range(PEER_TOPK):
                    ni = jnp.where(rank0 == float(a), n[a:a + 1, :], ni)
                ni_ref[h, :, ls] = ni
        return bad

    has_ties = jnp.max(route(break_ties=False)) > 0.0

    @pl.when(has_ties)
    def _():
        route(break_ties=True)


def peer_route(x, g, wq_t, sub_keys, *, tt):
    n = x.shape[0]
    fspec = pl.BlockSpec((PEER_HEADS, PEER_NKEYS, tt), lambda i: (0, 0, i))

    def fac(dtype):
        return jax.ShapeDtypeStruct((PEER_HEADS, PEER_NKEYS, n), dtype)

    return pl.pallas_call(
        _peer_route_kernel,
        grid=(n // tt,),
        in_specs=[pl.BlockSpec((tt, D_MODEL), lambda i: (i, 0)),
                  pl.BlockSpec((1, D_MODEL), lambda i: (0, 0)),
                  pl.BlockSpec(wq_t.shape, lambda i: (0, 0)),
                  pl.BlockSpec(sub_keys.shape, lambda i: (0, 0, 0))],
        out_specs=[pl.BlockSpec((tt, D_MODEL), lambda i: (i, 0)), fspec, fspec, fspec, fspec],
        out_shape=[jax.ShapeDtypeStruct((n, D_MODEL), BF16), fac(F32), fac(F32), fac(BF16), fac(BF16)],
        scratch_shapes=[pltpu.VMEM((PEER_HEADS * 2 * PEER_HALF, tt), BF16)],
        compiler_params=_cparams(("arbitrary",)),
        name="peer_route",
    )(x, g.reshape(1, D_MODEL), wq_t, sub_keys)


def _peer_dense_kernel(xn_ref, u_ref, vt_ref, e0_ref, ni_ref, e1_ref, rj_ref, x_ref, gf_ref,
                       yp_ref, ys_ref, acc_ref, *sub_refs, te, np_tiles):
    i_tok = pl.program_id(0)
    e = pl.program_id(1)
    nsub = te // PEER_SUB
    a_refs, p_refs = sub_refs[:nsub], sub_refs[nsub:]
    tt = xn_ref.shape[0]
    rows_per_sub = PEER_SUB // PEER_NKEYS
    zero = jnp.zeros((), BF16)

    @pl.when(e == 0)
    def _():
        acc_ref[...] = jnp.zeros_like(acc_ref)

    def all_rows(row):
        tile = jnp.broadcast_to(row, (BF16_SUBLANES, tt)).astype(BF16)
        return jnp.concatenate([tile] * (PEER_NKEYS // BF16_SUBLANES), axis=0)

    def pre_activations(s):
        rs = slice(s * PEER_SUB, (s + 1) * PEER_SUB)
        a_refs[s][...] = lax.dot_general(u_ref[rs, :], xn_ref[...], (((1,), (1,)), ((), ())),
                                         preferred_element_type=F32)

    def weighted_activations(s):
        for k in range(rows_per_sub):
            r = s * rows_per_sub + k
            rr = slice(k * PEER_NKEYS, (k + 1) * PEER_NKEYS)
            w = None
            for h in range(PEER_HEADS):
                wh = jnp.where(rj_ref[h] < all_rows(ni_ref[h, r:r + 1, :]),
                               all_rows(e0_ref[h, r:r + 1, :]) * e1_ref[h], zero)
                w = wh if w is None else w + wh
            a = a_refs[s][rr, :].astype(BF16)
            act = 0.5 * a * (1.0 + lax.erf(a * (2.0 ** -0.5)))
            p_refs[s][rr, :] = w * act

    pre_activations(0)
    for s in range(nsub):
        if s + 1 < nsub:
            pre_activations(s + 1)
        weighted_activations(s)
        rs = slice(s * PEER_SUB, (s + 1) * PEER_SUB)
        acc_ref[...] += jnp.dot(vt_ref[:, rs], p_refs[s][...], preferred_element_type=F32)

    @pl.when(e == pl.num_programs(1) - 1)
    def _():
        y = _rmsnorm_rows(x_ref[...] + jnp.transpose(acc_ref[...]), gf_ref[...])

        @pl.when(i_tok < np_tiles)
        def _():
            yp_ref[...] = y

        @pl.when(i_tok >= np_tiles)
        def _():
            ys_ref[...] = y


def peer_dense(xn, u, v_t, e0, ni, e1, rj, x, g_final, n_p, *, tt, te):
    n = xn.shape[0]
    np_tiles = n_p // tt
    nsub = te // PEER_SUB
    fspec = pl.BlockSpec((PEER_HEADS, PEER_NKEYS, tt), lambda i, e: (0, 0, i))
    rspec = pl.BlockSpec((PEER_HEADS, te // PEER_NKEYS, tt), lambda i, e: (0, e, i))
    return pl.pallas_call(
        functools.partial(_peer_dense_kernel, te=te, np_tiles=np_tiles),
        grid=(n // tt, PEER_EXPERTS // te),
        in_specs=[pl.BlockSpec((tt, D_MODEL), lambda i, e: (i, 0)),
                  pl.BlockSpec((te, D_MODEL), lambda i, e: (e, 0)),
                  pl.BlockSpec((D_MODEL, te), lambda i, e: (0, e)),
                  rspec, rspec, fspec, fspec,
                  pl.BlockSpec((tt, D_MODEL), lambda i, e: (i, 0)),
                  pl.BlockSpec((1, D_MODEL), lambda i, e: (0, 0))],
        out_specs=[pl.BlockSpec((tt, D_MODEL), lambda i, e: (jnp.minimum(i, np_tiles - 1), 0)),
                   pl.BlockSpec((tt, D_MODEL), lambda i, e: (jnp.maximum(i - np_tiles, 0), 0))],
        out_shape=[jax.ShapeDtypeStruct((n_p, D_MODEL), F32),
                   jax.ShapeDtypeStruct((n - n_p, D_MODEL), F32)],
        scratch_shapes=([pltpu.VMEM((D_MODEL, tt), F32)]
                        + [pltpu.VMEM((PEER_SUB, tt), F32)] * nsub
                        + [pltpu.VMEM((PEER_SUB, tt), BF16)] * nsub),
        compiler_params=_cparams(("arbitrary", "arbitrary")),
        name="peer_dense",
    )(xn, u, v_t, e0, ni, e1, rj, x, g_final.reshape(1, D_MODEL))


def _lane_row(vec, start, width=GATE_W):
    return jnp.zeros((1, width), F32).at[0, start:start + vec.shape[0]].set(vec.astype(F32))


def _rep_matrix(first_row, heads, lanes_per_head, rows=GATE_W):
    r = jnp.arange(rows)[:, None]
    c = jnp.arange(heads * lanes_per_head)[None, :]
    return ((c // lanes_per_head) + first_row == r).astype(BF16)


def _pad_tail(buf):
    return jnp.pad(buf, ((0, 0), (TAIL - (CONV_K - 1), 0), (0, 0)))


def kernel(x_prompt, x_sample, state_dn_conv, state_dn, state_ssm_conv, state_ssm, cache_mem_k, cache_mem_v,
           mem_prompt, g_mix, w_in, dn_conv_w, dn_A_log, dn_dt_bias, dn_norm_w, ssm_conv_w, ssm_conv_b,
           ssm_A_log, ssm_dt_bias, ssm_D, ssm_norm_w, w_out, g_xattn, g_mem, w_xq, w_mkv, w_xo, g_ffn, w_pq,
           peer_sub_keys, peer_u, peer_v, g_final):
    depth = g_mix.shape[0]
    assert depth == 1
    bp, lp, d = x_prompt.shape
    bs, ls, _ = x_sample.shape
    n_p, n_s = bp * lp, bs * ls
    n = n_p + n_s
    ncp = lp // CHUNK
    tm = 512
    x = (x_prompt.reshape(n_p, d), x_sample.reshape(n_s, d))

    wi = w_in[0]
    o_xbc = DN_QKV
    o_zdn = o_xbc + SSM_XBC
    o_zssm = o_zdn + DN_WIDTH
    o_small = o_zssm + SSM_WIDTH
    n_small = 2 * DN_HEADS + SSM_HEADS
    w_cat = jnp.concatenate([
        wi[:, :DN_QKV], wi[:, o_xbc:o_xbc + SSM_XBC],
        wi[:, o_small:o_small + n_small], jnp.zeros((d, COL_ZDN - COL_GATE - n_small), F32),
        wi[:, o_zdn:o_zdn + DN_WIDTH], wi[:, o_zssm:o_zssm + SSM_WIDTH]], axis=1).astype(BF16)

    proj = norm_matmul(x, g_mix[0], w_cat, tm=2 * tm, tn=1024)
    proj_c = proj.reshape(n // CHUNK, CHUNK, PROJ_COLS)
    proj_s = proj.reshape(n // ls, ls, PROJ_COLS)

    gdn_params = (dn_conv_w[0],
                  jnp.concatenate([_lane_row(dn_A_log[0], DN_HEADS), _lane_row(dn_dt_bias[0], DN_HEADS)], axis=0),
                  dn_norm_w[0].reshape(1, DN_DK),
                  _rep_matrix(0, DN_HEADS, DN_DK), _rep_matrix(DN_HEADS, DN_HEADS, DN_DK))
    ssd_params = (ssm_conv_w[0], ssm_conv_b[0].reshape(1, SSM_XBC),
                  jnp.concatenate([_lane_row(ssm_A_log[0], 2 * DN_HEADS), _lane_row(ssm_dt_bias[0], 2 * DN_HEADS)],
                                  axis=0),
                  jnp.repeat(ssm_D[0], SSM_HEADDIM).reshape(1, SSM_WIDTH),
                  ssm_norm_w[0].reshape(1, SSM_WIDTH),
                  _rep_matrix(2 * DN_HEADS, SSM_HEADS, SSM_HEADDIM), _rep_matrix(2 * DN_HEADS, SSM_HEADS, LANES))

    odn_p, p_dn = gdn_mixer(proj_c, 0, bp, ncp, CHUNK, 4, 1, gdn_params)
    odn_s, s_dn = gdn_mixer(proj_s, n_p // ls, bs, 1, ls, 1, 8, gdn_params,
                            buf=_pad_tail(state_dn_conv[0]), s0=state_dn[0])
    ossm_p, p_ss = ssd_mixer(proj_c, 0, bp, ncp, CHUNK, 4, 1, ssd_params)
    ossm_s, s_ss = ssd_mixer(proj_s, n_p // ls, bs, 1, ls, 1, 8, ssd_params,
                             buf=_pad_tail(state_ssm_conv[0]),
                             s0=state_ssm[0].reshape(bs, SSM_PAIRS, PAIR_W, SSM_STATE))
    wo = w_out[0].astype(BF16)
    x1 = mixer_out((odn_p.reshape(n_p, DN_WIDTH), odn_s.reshape(n_s, DN_WIDTH)),
                   (ossm_p.reshape(n_p, SSM_WIDTH), ossm_s.reshape(n_s, SSM_WIDTH)),
                   wo[:DN_WIDTH], wo[DN_WIDTH:], x, tm=tm)

    last_c = slice(ncp - 1, bp * ncp, ncp)
    hist_c = slice(CHUNK - (CONV_K - 1), CHUNK)
    hist_s = slice(ls - (CONV_K - 1), ls)
    p_dnc = proj_c[last_c, hist_c, COL_QKV:COL_QKV + DN_QKV]
    p_sc = proj_c[last_c, hist_c, COL_XBC:COL_XBC + SSM_XBC]
    s_dnc = proj_s[n_p // ls:, hist_s, COL_QKV:COL_QKV + DN_QKV]
    s_sc = proj_s[n_p // ls:, hist_s, COL_XBC:COL_XBC + SSM_XBC]

    mkv = norm_matmul(mem_prompt.reshape(bp * MEM_LEN, d), g_mem[0], w_mkv[0].astype(BF16), tm=tm, tn=1024)
    mkv3 = mkv.reshape(bp, MEM_LEN, 2 * d)
    q = norm_matmul(x1, g_xattn[0], w_xq[0].astype(BF16), tm=tm, tn=1024)
    tl = 512
    a_p = xattn_core(q.reshape(n // tl, tl, d), 0, bp, lp // tl, tl, mkv3, mkv3, kv_col=(0, 1))
    a_s = xattn_cache(q.reshape(n // ls, ls, d), n_p // ls, bs, ls, 4, cache_mem_k, cache_mem_v)
    x2 = matmul_res((a_p.reshape(n_p, d), a_s.reshape(n_s, d)), w_xo[0].astype(BF16), x1, tm=tm)

    xn, e0, ni, e1, rj = peer_route(x2, g_ffn[0], jnp.transpose(w_pq[0]).astype(BF16),
                                    peer_sub_keys[0].astype(BF16), tt=256)
    y_p, y_s = peer_dense(xn, peer_u[0].astype(BF16), jnp.transpose(peer_v[0]).astype(BF16),
                          e0, ni, e1, rj, x2, g_final, n_p, tt=512, te=2048)

    def heads(m):
        return m.reshape(bp, MEM_LEN, MEM_HEADS, MEM_HD)[None]

    return (y_p.reshape(bp, lp, d), y_s.reshape(bs, ls, d),
            p_dnc[None], p_dn[None], p_sc[None], p_ss.reshape(bp, SSM_HEADS, SSM_HEADDIM, SSM_STATE)[None],
            heads(mkv3[:, :, :d]), heads(mkv3[:, :, d:]),
            s_dnc[None], s_dn[None], s_sc[None], s_ss.reshape(bs, SSM_HEADS, SSM_HEADDIM, SSM_STATE)[None])
```

```python
import functools

import jax
import jax.numpy as jnp
from jax import lax
from jax.experimental import pallas as pl
from jax.experimental.pallas import tpu as pltpu

F32 = jnp.float32
BF16 = jnp.bfloat16
EPS = 1e-6

D_MODEL = 1024
CONV_K = 4
CHUNK = 64
DN_HEADS = 8
DN_DK = 128
DN_WIDTH = 1024
DN_QKV = 3072
SSM_HEADS = 16
SSM_HEADDIM = 64
SSM_STATE = 128
SSM_WIDTH = 1024
SSM_XBC = 1536
MEM_LEN = 256
MEM_HEADS = 4
MEM_HD = 256
PEER_HEADS = 8
PEER_NKEYS = 128
PEER_EXPERTS = PEER_NKEYS * PEER_NKEYS
PEER_TOPK = 16
PEER_HALF = 128

COL_QKV = 0
COL_XBC = 3072
COL_GATE = 4608
COL_ZDN = 5120
COL_ZSSM = 6144
PROJ_COLS = 7168
GATE_W = 128
TAIL = 8
INV_BLOCK = 16
NOT_TOP = 99.0
LANES = 128
BF16_SUBLANES = 16
PEER_SUB = 1024

VMEM_LIMIT = 56 * 1024 * 1024


def _cparams(sem):
    return pltpu.CompilerParams(dimension_semantics=sem, vmem_limit_bytes=VMEM_LIMIT)


def _bdot(a, b):
    return jnp.dot(a.astype(BF16), b.astype(BF16), preferred_element_type=F32)


def _bdot_nt(a, b):
    return lax.dot_general(a.astype(BF16), b.astype(BF16), (((1,), (1,)), ((), ())),
                           preferred_element_type=F32)


def _bdot_tn(a, b):
    return lax.dot_general(a.astype(BF16), b.astype(BF16), (((0,), (0,)), ((), ())),
                           preferred_element_type=F32)


def _split3(x):
    hi = x.astype(BF16)
    r = x - hi.astype(F32)
    mid = r.astype(BF16)
    lo = (r - mid.astype(F32)).astype(BF16)
    return hi, mid, lo


def _dot_sel_rhs(x, sel):
    hi, mid, lo = _split3(x)
    d = functools.partial(jnp.dot, preferred_element_type=F32)
    return d(hi, sel) + d(mid, sel) + d(lo, sel)


def _dot_sel_lhs(sel, x):
    hi, mid, lo = _split3(x)
    d = functools.partial(jnp.dot, preferred_element_type=F32)
    return d(sel, hi) + d(sel, mid) + d(sel, lo)


def _silu(x):
    return x * jax.nn.sigmoid(x)


def _softplus(x):
    return jnp.maximum(x, 0.0) + jnp.log1p(jnp.exp(-jnp.abs(x)))


def _unit_lower_inverse(lms, n):
    row = lax.broadcasted_iota(jnp.int32, (n, n), 0)
    col = lax.broadcasted_iota(jnp.int32, (n, n), 1)
    eye = (row == col).astype(F32)

    def nilpotent_inverse(xs, index):
        invs = [eye - x for x in xs]
        ps = xs
        k = 2
        while k < index:
            ps = [_bdot(p, p) for p in ps]
            invs = [inv + _bdot(inv, p) for inv, p in zip(invs, ps)]
            k *= 2
        return invs

    if n <= INV_BLOCK:
        return nilpotent_inverse(lms, n)
    shift = INV_BLOCK.bit_length() - 1
    same = jnp.right_shift(row, shift) == jnp.right_shift(col, shift)
    dinvs = nilpotent_inverse([jnp.where(same, lm, 0.0) for lm in lms], INV_BLOCK)
    fs = [_bdot(dinv, jnp.where(same, 0.0, lm)) for dinv, lm in zip(dinvs, lms)]
    finvs = nilpotent_inverse(fs, n // INV_BLOCK)
    return [_bdot(finv, dinv) for finv, dinv in zip(finvs, dinvs)]


def _causal_conv(x, xp_ref, w, lc):
    xp_ref[TAIL:TAIL + lc, :] = x
    y = x * w[3:4, :]
    for s in range(1, CONV_K):
        y = y + xp_ref[TAIL - s:TAIL - s + lc, :] * w[CONV_K - 1 - s:CONV_K - s, :]
    xp_ref[0:TAIL, :] = xp_ref[lc:lc + TAIL, :]
    return y


def _decay_matrix(col, row, lower_incl):
    diff = jnp.where(lower_incl, col - row, 0.0)
    return jnp.where(lower_incl, jnp.exp(diff), 0.0)


def _pair_specs(tm, k, np_tiles, two_axes):
    if two_axes:
        return [pl.BlockSpec((tm, k), lambda i, j: (jnp.minimum(i, np_tiles - 1), 0)),
                pl.BlockSpec((tm, k), lambda i, j: (jnp.maximum(i - np_tiles, 0), 0))]
    return [pl.BlockSpec((tm, k), lambda i: (jnp.minimum(i, np_tiles - 1), 0)),
            pl.BlockSpec((tm, k), lambda i: (jnp.maximum(i - np_tiles, 0), 0))]


def _pick(is_prompt, p_ref, s_ref):
    return jnp.where(is_prompt, p_ref[...], s_ref[...])


def _rmsnorm_rows(x, g):
    ms = jnp.mean(x * x, axis=-1, keepdims=True)
    return x * lax.rsqrt(ms + EPS) * g


def _norm_matmul_kernel(*refs, np_tiles):
    if np_tiles is None:
        x_ref, g_ref, w_ref, o_ref, xn_ref = refs
    else:
        xp_ref, xs_ref, g_ref, w_ref, o_ref, xn_ref = refs
        is_prompt = pl.program_id(0) < np_tiles

    @pl.when(pl.program_id(1) == 0)
    def _():
        x = x_ref[...] if np_tiles is None else _pick(is_prompt, xp_ref, xs_ref)
        xn_ref[...] = _rmsnorm_rows(x, g_ref[...]).astype(BF16)

    o_ref[...] = jnp.dot(xn_ref[...], w_ref[...], preferred_element_type=F32).astype(o_ref.dtype)


def norm_matmul(x, g, w, *, tm, tn, out_dtype=F32):
    pair = isinstance(x, tuple)
    n = sum(a.shape[0] for a in x) if pair else x.shape[0]
    k, m = w.shape
    assert n % tm == 0 and m % tn == 0
    if pair:
        assert x[0].shape[0] % tm == 0
        np_tiles = x[0].shape[0] // tm
        x_specs = _pair_specs(tm, k, np_tiles, True)
        x_args = list(x)
    else:
        np_tiles = None
        x_specs = [pl.BlockSpec((tm, k), lambda i, j: (i, 0))]
        x_args = [x]
    return pl.pallas_call(
        functools.partial(_norm_matmul_kernel, np_tiles=np_tiles),
        grid=(n // tm, m // tn),
        in_specs=x_specs + [pl.BlockSpec((1, k), lambda i, j: (0, 0)),
                            pl.BlockSpec((k, tn), lambda i, j: (0, j))],
        out_specs=pl.BlockSpec((tm, tn), lambda i, j: (i, j)),
        out_shape=jax.ShapeDtypeStruct((n, m), out_dtype),
        scratch_shapes=[pltpu.VMEM((tm, k), BF16)],
        compiler_params=_cparams(("arbitrary", "arbitrary")),
        name="norm_matmul",
    )(*x_args, g.reshape(1, k), w)


def _mixer_out_kernel(dp_ref, ds_ref, sp_ref, ss_ref, w1_ref, w2_ref, xp_ref, xs_ref, o_ref, *, np_tiles):
    is_prompt = pl.program_id(0) < np_tiles
    a1 = _pick(is_prompt, dp_ref, ds_ref).astype(BF16)
    a2 = _pick(is_prompt, sp_ref, ss_ref).astype(BF16)
    acc = jnp.dot(a1, w1_ref[...], preferred_element_type=F32)
    acc = acc + jnp.dot(a2, w2_ref[...], preferred_element_type=F32)
    o_ref[...] = _pick(is_prompt, xp_ref, xs_ref) + acc


def mixer_out(o_dn, o_ssm, w1, w2, x, *, tm):
    n = x[0].shape[0] + x[1].shape[0]
    k = w1.shape[0]
    m = w1.shape[1]
    np_tiles = x[0].shape[0] // tm
    wspec = pl.BlockSpec((k, m), lambda i: (0, 0))
    return pl.pallas_call(
        functools.partial(_mixer_out_kernel, np_tiles=np_tiles),
        grid=(n // tm,),
        in_specs=(_pair_specs(tm, k, np_tiles, False) + _pair_specs(tm, k, np_tiles, False)
                  + [wspec, wspec] + _pair_specs(tm, m, np_tiles, False)),
        out_specs=pl.BlockSpec((tm, m), lambda i: (i, 0)),
        out_shape=jax.ShapeDtypeStruct((n, m), F32),
        compiler_params=_cparams(("arbitrary",)),
        name="mixer_out",
    )(*o_dn, *o_ssm, w1, w2, *x)


def _matmul_res_kernel(ap_ref, as_ref, w_ref, r_ref, o_ref, *, np_tiles):
    a = _pick(pl.program_id(0) < np_tiles, ap_ref, as_ref).astype(BF16)
    o_ref[...] = r_ref[...] + jnp.dot(a, w_ref[...], preferred_element_type=F32)


def matmul_res(a, w, res, *, tm):
    n, m = res.shape
    k = w.shape[0]
    np_tiles = a[0].shape[0] // tm
    return pl.pallas_call(
        functools.partial(_matmul_res_kernel, np_tiles=np_tiles),
        grid=(n // tm,),
        in_specs=_pair_specs(tm, k, np_tiles, False) + [pl.BlockSpec((k, m), lambda i: (0, 0)),
                                                        pl.BlockSpec((tm, m), lambda i: (i, 0))],
        out_specs=pl.BlockSpec((tm, m), lambda i: (i, 0)),
        out_shape=jax.ShapeDtypeStruct((n, m), F32),
        compiler_params=_cparams(("arbitrary",)),
        name="matmul_res",
    )(*a, w, res)


def _gdn_kernel(*refs, lc, ng, nbk, has_state):
    refs = list(refs)

    def take(k):
        out = refs[:k]
        del refs[:k]
        return out

    qkv_refs, gate_refs, z_refs = take(ng), take(ng), take(ng)
    if has_state:
        buf_ref, s0_ref = take(2)
    cw_ref, gp_ref, nw_ref, repb_ref, repg_ref, o_ref, s_ref, xp_ref = take(8)
    nseq = ng * nbk
    where = [divmod(j, nbk) for j in range(nseq)]

    @pl.when(pl.program_id(1) == 0)
    def _():
        if has_state:
            xp_ref[:, 0:TAIL, :] = buf_ref[...]
            s_ref[...] = s0_ref[...]
        else:
            xp_ref[:, 0:TAIL, :] = jnp.zeros((nseq, TAIL, DN_QKV), F32)
            s_ref[...] = jnp.zeros(s_ref.shape, F32)

    row = lax.broadcasted_iota(jnp.int32, (lc, lc), 0)
    col = lax.broadcasted_iota(jnp.int32, (lc, lc), 1)
    incl = row >= col
    strict = row > col
    tri = incl.astype(BF16)

    seqs = []
    for j, (g_, k_) in enumerate(where):
        qkv = _silu(_causal_conv(qkv_refs[g_][k_], xp_ref.at[j], cw_ref[...], lc))
        gate = gate_refs[g_][k_]
        beta_all = jax.nn.sigmoid(gate)
        g_all = -jnp.exp(gp_ref[0:1, :]) * _softplus(gate + gp_ref[1:2, :])
        gc = _dot_sel_lhs(tri, g_all)
        gc_b = _dot_sel_rhs(gc, repg_ref[...])
        gc_last_b = gc_b[lc - 1:lc, :]
        seqs.append(dict(qkv=qkv, gc_t=jnp.transpose(gc), gc_b=gc_b,
                         beta_b=_dot_sel_rhs(beta_all, repb_ref[...]),
                         egc_b=jnp.exp(gc_b), ekd_b=jnp.exp(gc_last_b - gc_b), egt_b=jnp.exp(gc_last_b)))

    units = [(j, h) for j in range(nseq) for h in range(DN_HEADS)]
    qs, ks, vbs, kbegs, decays = [], [], [], [], []
    for j, h in units:
        sq = seqs[j]
        sl = slice(h * DN_DK, (h + 1) * DN_DK)
        qh = sq["qkv"][:, h * DN_DK:(h + 1) * DN_DK]
        kh = sq["qkv"][:, DN_WIDTH + h * DN_DK:DN_WIDTH + (h + 1) * DN_DK]
        vh = sq["qkv"][:, 2 * DN_WIDTH + h * DN_DK:2 * DN_WIDTH + (h + 1) * DN_DK]
        qh = qh * lax.rsqrt(jnp.sum(qh * qh, axis=-1, keepdims=True) + EPS) * (DN_DK ** -0.5)
        kh = kh * lax.rsqrt(jnp.sum(kh * kh, axis=-1, keepdims=True) + EPS)
        bh = sq["beta_b"][:, sl]
        qs.append(qh)
        ks.append(kh)
        vbs.append(vh * bh)
        kbegs.append((kh * bh, sq["egc_b"][:, sl]))
        decays.append(_decay_matrix(sq["gc_b"][:, h * DN_DK:h * DN_DK + lc],
                                    sq["gc_t"][DN_HEADS + h:DN_HEADS + h + 1, :], incl))
    lowers = [jnp.where(strict, _bdot_nt(kb, kh) * dec, 0.0)
              for (kb, _), kh, dec in zip(kbegs, ks, decays)]
    attns = [_bdot_nt(qh, kh) * dec for qh, kh, dec in zip(qs, ks, decays)]
    tinvs = _unit_lower_inverse(lowers, lc)
    us = [_bdot(t, vb) for t, vb in zip(tinvs, vbs)]
    ws = [_bdot(t, kb * eg) for t, (kb, eg) in zip(tinvs, kbegs)]
    states = [s_ref[j, h] for j, h in units]
    v_news = [u - _bdot(w, s) for u, w, s in zip(us, ws, states)]
    os_ = [_bdot(qh * eg, s) + _bdot(a, vn)
           for qh, (_, eg), s, a, vn in zip(qs, kbegs, states, attns, v_news)]
    for (j, h), kh, s, vn, o in zip(units, ks, states, v_news, os_):
        g_, k_ = where[j]
        sl = slice(h * DN_DK, (h + 1) * DN_DK)
        sq = seqs[j]
        s_ref[j, h] = s * sq["egt_b"][:, sl] + _bdot_tn(kh * sq["ekd_b"][:, sl], vn)
        o = o * lax.rsqrt(jnp.mean(o * o, axis=-1, keepdims=True) + EPS) * nw_ref[...]
        o_ref[j, 0, :, sl] = o * _silu(z_refs[g_][k_, :, sl])


def gdn_mixer(proj3, row0, nseq, nchunk, lc, ng, nbk, params, buf=None, s0=None):
    has_state = buf is not None
    step = ng * nbk
    assert nseq % step == 0 and row0 % nbk == 0 and (nbk == 1 or nchunk == 1)

    def proj_specs(width, col_block):
        return [pl.BlockSpec((nbk, lc, width),
                             lambda b, c, g_=g_: (row0 // nbk + (b * ng + g_) * nchunk + c, 0, col_block))
                for g_ in range(ng)]

    in_specs = (proj_specs(DN_QKV, COL_QKV // DN_QKV) + proj_specs(GATE_W, COL_GATE // GATE_W)
                + proj_specs(DN_WIDTH, COL_ZDN // DN_WIDTH))
    args = [proj3] * (3 * ng)
    state_spec = pl.BlockSpec((step, DN_HEADS, DN_DK, DN_DK), lambda b, c: (b, 0, 0, 0))
    if has_state:
        in_specs += [pl.BlockSpec((step, TAIL, DN_QKV), lambda b, c: (b, 0, 0)), state_spec]
        args += [buf, s0]
    for prm in params:
        in_specs.append(pl.BlockSpec(prm.shape, lambda b, c: (0, 0)))
        args.append(prm)
    return pl.pallas_call(
        functools.partial(_gdn_kernel, lc=lc, ng=ng, nbk=nbk, has_state=has_state),
        grid=(nseq // step, nchunk),
        in_specs=in_specs,
        out_specs=[pl.BlockSpec((step, 1, lc, DN_WIDTH), lambda b, c: (b, c, 0, 0)), state_spec],
        out_shape=[jax.ShapeDtypeStruct((nseq, nchunk, lc, DN_WIDTH), F32),
                   jax.ShapeDtypeStruct((nseq, DN_HEADS, DN_DK, DN_DK), F32)],
        scratch_shapes=[pltpu.VMEM((step, TAIL + lc, DN_QKV), F32)],
        compiler_params=_cparams(("arbitrary", "arbitrary")),
        name="gdn_mixer_state" if has_state else "gdn_mixer",
    )(*args)


SSM_PAIRS = SSM_HEADS // 2
PAIR_W = 2 * SSM_HEADDIM
GROUP_W = SSM_WIDTH // 2


def _ssd_kernel(*refs, lc, ng, nbk, has_state):
    refs = list(refs)

    def take(k):
        out = refs[:k]
        del refs[:k]
        return out

    xbc_refs, gate_refs, z_refs = take(ng), take(ng), take(ng)
    if has_state:
        buf_ref, s0_ref = take(2)
    (cw_ref, cb_ref, gp_ref, dd_ref, nw_ref, reps_ref, repw_ref, o_ref, s_ref, xp_ref) = take(10)
    nseq = ng * nbk
    where = [divmod(j, nbk) for j in range(nseq)]

    @pl.when(pl.program_id(1) == 0)
    def _():
        if has_state:
            xp_ref[:, 0:TAIL, :] = buf_ref[...]
            s_ref[...] = s0_ref[...]
        else:
            xp_ref[:, 0:TAIL, :] = jnp.zeros((nseq, TAIL, SSM_XBC), F32)
            s_ref[...] = jnp.zeros(s_ref.shape, F32)

    row = lax.broadcasted_iota(jnp.int32, (lc, lc), 0)
    col = lax.broadcasted_iota(jnp.int32, (lc, lc), 1)
    incl = row >= col
    tri = incl.astype(BF16)
    first_head = lax.broadcasted_iota(jnp.int32, (lc, PAIR_W), 1) < SSM_HEADDIM

    seqs = []
    for j, (g_, k_) in enumerate(where):
        xbc = _silu(_causal_conv(xbc_refs[g_][k_], xp_ref.at[j], cw_ref[...], lc) + cb_ref[...])
        xs = xbc[:, :SSM_WIDTH]
        dt = _softplus(gate_refs[g_][k_] + gp_ref[1:2, :])
        acs = _dot_sel_lhs(tri, dt * (-jnp.exp(gp_ref[0:1, :])))
        acs_b = _dot_sel_rhs(acs, reps_ref[...])
        acs_w = _dot_sel_rhs(acs, repw_ref[...])
        seqs.append(dict(xbc=xbc, xs=xs, acs_t=jnp.transpose(acs), acs_w=acs_w,
                         xd=xs * _dot_sel_rhs(dt, reps_ref[...]),
                         eacs_b=jnp.exp(acs_b), ends_b=jnp.exp(acs_b[lc - 1:lc, :] - acs_b),
                         ecd_w=jnp.exp(acs_w[lc - 1:lc, :])))

    def bmat(sq, g):
        return sq["xbc"][:, SSM_WIDTH + g * SSM_STATE:SSM_WIDTH + (g + 1) * SSM_STATE]

    def cmat(sq, g):
        o = SSM_WIDTH + 2 * SSM_STATE
        return sq["xbc"][:, o + g * SSM_STATE:o + (g + 1) * SSM_STATE]

    cbs = [[_bdot_nt(cmat(sq, g), bmat(sq, g)) for g in range(2)] for sq in seqs]
    units = [(j, p) for j in range(nseq) for p in range(SSM_PAIRS)]
    lmats = [[_decay_matrix(seqs[j]["acs_w"][:, hh * LANES:hh * LANES + lc],
                            seqs[j]["acs_t"][2 * DN_HEADS + hh:2 * DN_HEADS + hh + 1, :], incl)
              for hh in (2 * p, 2 * p + 1)] for j, p in units]
    prevs = [s_ref[j, p] for j, p in units]
    y_diag, y_off, sts = [], [], []
    for (j, p), lm, prev in zip(units, lmats, prevs):
        sq = seqs[j]
        g = p // (SSM_PAIRS // 2)
        psl = slice(p * PAIR_W, (p + 1) * PAIR_W)
        xd_p = sq["xd"][:, psl]
        y_diag.append([_bdot(cbs[j][g] * lm[0], xd_p), _bdot(cbs[j][g] * lm[1], xd_p)])
        y_off.append(_bdot_nt(cmat(sq, g), prev))
        sts.append(_bdot_tn(xd_p * sq["ends_b"][:, psl], bmat(sq, g)))
    ys = []
    ssq = [[None, None] for _ in range(nseq)]
    for (j, p), yd, yo, st, prev in zip(units, y_diag, y_off, sts, prevs):
        sq = seqs[j]
        g_, k_ = where[j]
        g = p // (SSM_PAIRS // 2)
        psl = slice(p * PAIR_W, (p + 1) * PAIR_W)
        cd = jnp.concatenate(
            [jnp.broadcast_to(sq["ecd_w"][:, (2 * p) * LANES:(2 * p + 1) * LANES], (SSM_HEADDIM, SSM_STATE)),
             jnp.broadcast_to(sq["ecd_w"][:, (2 * p + 1) * LANES:(2 * p + 2) * LANES], (SSM_HEADDIM, SSM_STATE))],
            axis=0)
        s_ref[j, p] = prev * cd + st
        y = jnp.where(first_head, yd[0], yd[1]) + yo * sq["eacs_b"][:, psl] + dd_ref[:, psl] * sq["xs"][:, psl]
        y = y * _silu(z_refs[g_][k_, :, psl])
        s2 = jnp.sum(y * y, axis=-1, keepdims=True)
        ssq[j][g] = s2 if ssq[j][g] is None else ssq[j][g] + s2
        ys.append(y)
    for (j, p), y in zip(units, ys):
        g = p // (SSM_PAIRS // 2)
        psl = slice(p * PAIR_W, (p + 1) * PAIR_W)
        o_ref[j, 0, :, psl] = y * lax.rsqrt(ssq[j][g] * (1.0 / GROUP_W) + EPS) * nw_ref[:, psl]


def ssd_mixer(proj3, row0, nseq, nchunk, lc, ng, nbk, params, buf=None, s0=None):
    has_state = buf is not None
    step = ng * nbk
    assert nseq % step == 0 and row0 % nbk == 0 and (nbk == 1 or nchunk == 1)

    def proj_specs(width, col_block):
        return [pl.BlockSpec((nbk, lc, width),
                             lambda b, c, g_=g_: (row0 // nbk + (b * ng + g_) * nchunk + c, 0, col_block))
                for g_ in range(ng)]

    in_specs = (proj_specs(SSM_XBC, COL_XBC // SSM_XBC) + proj_specs(GATE_W, COL_GATE // GATE_W)
                + proj_specs(SSM_WIDTH, COL_ZSSM // SSM_WIDTH))
    args = [proj3] * (3 * ng)
    state_spec = pl.BlockSpec((step, SSM_PAIRS, PAIR_W, SSM_STATE), lambda b, c: (b, 0, 0, 0))
    if has_state:
        in_specs += [pl.BlockSpec((step, TAIL, SSM_XBC), lambda b, c: (b, 0, 0)), state_spec]
        args += [buf, s0]
    for prm in params:
        in_specs.append(pl.BlockSpec(prm.shape, lambda b, c: (0, 0)))
        args.append(prm)
    return pl.pallas_call(
        functools.partial(_ssd_kernel, lc=lc, ng=ng, nbk=nbk, has_state=has_state),
        grid=(nseq // step, nchunk),
        in_specs=in_specs,
        out_specs=[pl.BlockSpec((step, 1, lc, SSM_WIDTH), lambda b, c: (b, c, 0, 0)), state_spec],
        out_shape=[jax.ShapeDtypeStruct((nseq, nchunk, lc, SSM_WIDTH), F32),
                   jax.ShapeDtypeStruct((nseq, SSM_PAIRS, PAIR_W, SSM_STATE), F32)],
        scratch_shapes=[pltpu.VMEM((step, TAIL + lc, SSM_XBC), F32)],
        compiler_params=_cparams(("arbitrary", "arbitrary")),
        name="ssd_mixer_state" if has_state else "ssd_mixer",
    )(*args)


def _xattn_kernel(q_ref, k_ref, v_ref, o_ref):
    for h in range(MEM_HEADS):
        sl = slice(h * MEM_HD, (h + 1) * MEM_HD)
        k = k_ref[0, :, sl]
        v = v_ref[0, :, sl]
        s = _bdot_nt(q_ref[0, :, sl], k) * (MEM_HD ** -0.5)
        s = s - jnp.max(s, axis=-1, keepdims=True)
        p = jnp.exp(s)
        p = p / jnp.sum(p, axis=-1, keepdims=True)
        o_ref[0, :, sl] = _bdot(p, v)


def _xattn_cache_kernel(q_ref, k_ref, v_ref, o_ref, *, tl, nb):
    shape = (MEM_HEADS * tl, MEM_LEN * MEM_HEADS)
    row_head = jnp.right_shift(lax.broadcasted_iota(jnp.int32, shape, 0), tl.bit_length() - 1)
    col_head = jnp.bitwise_and(lax.broadcasted_iota(jnp.int32, shape, 1), MEM_HEADS - 1)
    own_head = row_head == col_head
    for j in range(nb):
        k2 = k_ref[j].reshape(MEM_LEN * MEM_HEADS, MEM_HD)
        v2 = v_ref[j].reshape(MEM_LEN * MEM_HEADS, MEM_HD)
        q4 = jnp.concatenate([q_ref[j, :, h * MEM_HD:(h + 1) * MEM_HD] for h in range(MEM_HEADS)], axis=0)
        s = jnp.where(own_head, _bdot_nt(q4, k2) * (MEM_HD ** -0.5), -jnp.inf)
        s = s - jnp.max(s, axis=-1, keepdims=True)
        p = jnp.exp(s)
        p = p / jnp.sum(p, axis=-1, keepdims=True)
        o4 = _bdot(p, v2)
        for h in range(MEM_HEADS):
            o_ref[j, :, h * MEM_HD:(h + 1) * MEM_HD] = o4[h * tl:(h + 1) * tl, :]


def xattn_cache(q3, row0, nseq, tl, nb, cache_k, cache_v):
    assert tl & (tl - 1) == 0 and MEM_HEADS & (MEM_HEADS - 1) == 0 and nseq % nb == 0 and row0 % nb == 0
    kv_spec = pl.BlockSpec((None, nb, MEM_LEN, MEM_HEADS, MEM_HD), lambda b: (0, b, 0, 0, 0))
    return pl.pallas_call(
        functools.partial(_xattn_cache_kernel, tl=tl, nb=nb),
        grid=(nseq // nb,),
        in_specs=[pl.BlockSpec((nb, tl, D_MODEL), lambda b: (row0 // nb + b, 0, 0)), kv_spec, kv_spec],
        out_specs=pl.BlockSpec((nb, tl, D_MODEL), lambda b: (b, 0, 0)),
        out_shape=jax.ShapeDtypeStruct((nseq, tl, D_MODEL), F32),
        compiler_params=_cparams(("arbitrary",)),
        name="xattn_cache",
    )(q3, cache_k, cache_v)


def xattn_core(q3, row0, nseq, ntile, tl, mem_k, mem_v, kv_col=(0, 0)):
    kv_specs = [pl.BlockSpec((1, MEM_LEN, D_MODEL), lambda b, t, c_=c_: (b, 0, c_)) for c_ in kv_col]
    return pl.pallas_call(
        _xattn_kernel,
        grid=(nseq, ntile),
        in_specs=[pl.BlockSpec((1, tl, D_MODEL), lambda b, t: (row0 + b * ntile + t, 0, 0))] + kv_specs,
        out_specs=pl.BlockSpec((1, tl, D_MODEL), lambda b, t: (b * ntile + t, 0, 0)),
        out_shape=jax.ShapeDtypeStruct((nseq * ntile, tl, D_MODEL), F32),
        compiler_params=_cparams(("arbitrary", "arbitrary")),
        name="xattn_core",
    )(q3, mem_k, mem_v)


MAGNITUDE_BITS = 0x7FFFFFFF
NEG_INF_KEY = -0x800000 ^ MAGNITUDE_BITS
RANK0_MARK = NEG_INF_KEY - 1


def _order_key(x, inverse=False):
    b = x if inverse else lax.bitcast_convert_type(x, jnp.int32)
    k = jnp.where(b < 0, jnp.bitwise_xor(b, MAGNITUDE_BITS), b)
    return lax.bitcast_convert_type(k, F32) if inverse else k


def _top16_rows(s, break_ties):
    n = s.shape[0]
    v = _order_key(s + 0.0)
    vals = []
    for a in range(PEER_TOPK):
        m = jnp.max(v, axis=0, keepdims=True)
        hit = v == m
        if break_ties:
            iota = lax.broadcasted_iota(jnp.int32, s.shape, 0)
            hit = iota == jnp.min(jnp.where(hit, iota, n), axis=0, keepdims=True)
        v = jnp.where(hit, RANK0_MARK - a, v)
        vals.append(m)
    rank = jnp.where(v <= RANK0_MARK, (RANK0_MARK - v).astype(F32), NOT_TOP)
    return _order_key(jnp.concatenate(vals, axis=0), inverse=True), rank


def _pair_top16(s0v, s1v):
    t = s0v.shape[1]
    iota = lax.broadcasted_iota(jnp.int32, (PEER_TOPK, t), 0).astype(F32)
    n = jnp.zeros((PEER_TOPK, t), F32)
    front = s0v + s1v[0:1, :]
    top = s0v[0:1, :] + s1v[0:1, :]
    z = jnp.zeros((1, t), F32)
    for _ in range(PEER_TOPK):
        m = jnp.max(front, axis=0, keepdims=True)
        a_star = jnp.min(jnp.where(front == m, iota, float(PEER_TOPK)), axis=0, keepdims=True)
        hit = iota == a_star
        z = z + jnp.exp(m - top)
        n = jnp.where(hit, n + 1.0, n)
        nxt = jnp.full((PEER_TOPK, t), -jnp.inf, F32)
        for b in range(1, PEER_TOPK):
            nxt = jnp.where(n == float(b), s1v[b:b + 1, :], nxt)
        front = jnp.where(hit, s0v + nxt, front)
    return n, z


def _peer_route_kernel(x_ref, g_ref, wq_ref, sk_ref, xn_ref, e0_ref, ni_ref, e1_ref, rj_ref, q_ref):
    xn = _rmsnorm_rows(x_ref[...], g_ref[...]).astype(BF16)
    xn_ref[...] = xn
    q_ref[...] = lax.dot_general(wq_ref[...], xn, (((1,), (1,)), ((), ())),
                                 preferred_element_type=F32).astype(BF16)
    tt = x_ref.shape[0]

    def route(break_ties):
        bad = jnp.zeros((1, LANES), F32)
        for lb in range(tt // LANES):
            ls = slice(lb * LANES, (lb + 1) * LANES)
            for h in range(PEER_HEADS):
                def ranked_half(c):
                    r0 = (2 * h + c) * PEER_HALF
                    s = jnp.dot(sk_ref[c], q_ref[r0:r0 + PEER_HALF, ls],
                                preferred_element_type=F32)
                    vals, rank = _top16_rows(s, break_ties)
                    inside = rank < float(PEER_TOPK)
                    ex = jnp.where(inside, jnp.exp(jnp.where(inside, s - vals[0:1, :], 0.0)), 0.0)
                    count = jnp.sum(jnp.where(inside, 1.0, 0.0), axis=0, keepdims=True)
                    return vals, rank, ex, jnp.abs(count - float(PEER_TOPK))

                s1v, rank1, e1, off1 = ranked_half(1)
                e1_ref[h, :, ls] = e1.astype(BF16)
                rj_ref[h, :, ls] = rank1.astype(BF16)
                s0v, rank0, e0, off0 = ranked_half(0)
                if not break_ties:
                    bad = jnp.maximum(bad, jnp.maximum(off0, off1))
                n, z = _pair_top16(s0v, s1v)
                e0_ref[h, :, ls] = e0 / z
                ni = jnp.zeros_like(rank0)
                for a in range(PEER_TOPK):
                    ni = jnp.where(rank0 == float(a), n[a:a + 1, :], ni)
                ni_ref[h, :, ls] = ni
        return bad

    has_ties = jnp.max(route(break_ties=False)) > 0.0

    @pl.when(has_ties)
    def _():
        route(break_ties=True)


def peer_route(x, g, wq_t, sub_keys, *, tt):
    n = x.shape[0]
    fspec = pl.BlockSpec((PEER_HEADS, PEER_NKEYS, tt), lambda i: (0, 0, i))

    def fac(dtype):
        return jax.ShapeDtypeStruct((PEER_HEADS, PEER_NKEYS, n), dtype)

    return pl.pallas_call(
        _peer_route_kernel,
        grid=(n // tt,),
        in_specs=[pl.BlockSpec((tt, D_MODEL), lambda i: (i, 0)),
                  pl.BlockSpec((1, D_MODEL), lambda i: (0, 0)),
                  pl.BlockSpec(wq_t.shape, lambda i: (0, 0)),
                  pl.BlockSpec(sub_keys.shape, lambda i: (0, 0, 0))],
        out_specs=[pl.BlockSpec((tt, D_MODEL), lambda i: (i, 0)), fspec, fspec, fspec, fspec],
        out_shape=[jax.ShapeDtypeStruct((n, D_MODEL), BF16), fac(F32), fac(F32), fac(BF16), fac(BF16)],
        scratch_shapes=[pltpu.VMEM((PEER_HEADS * 2 * PEER_HALF, tt), BF16)],
        compiler_params=_cparams(("arbitrary",)),
        name="peer_route",
    )(x, g.reshape(1, D_MODEL), wq_t, sub_keys)


def _peer_dense_kernel(xn_ref, u_ref, vt_ref, e0_ref, ni_ref, e1_ref, rj_ref, x_ref, gf_ref,
                       yp_ref, ys_ref, acc_ref, *sub_refs, te, np_tiles):
    i_tok = pl.program_id(0)
    e = pl.program_id(1)
    nsub = te // PEER_SUB
    a_refs, p_refs = sub_refs[:nsub], sub_refs[nsub:]
    tt = xn_ref.shape[0]
    rows_per_sub = PEER_SUB // PEER_NKEYS
    zero = jnp.zeros((), BF16)

    @pl.when(e == 0)
    def _():
        acc_ref[...] = jnp.zeros_like(acc_ref)

    def all_rows(row):
        tile = jnp.broadcast_to(row, (BF16_SUBLANES, tt)).astype(BF16)
        return jnp.concatenate([tile] * (PEER_NKEYS // BF16_SUBLANES), axis=0)

    def pre_activations(s):
        rs = slice(s * PEER_SUB, (s + 1) * PEER_SUB)
        a_refs[s][...] = lax.dot_general(u_ref[rs, :], xn_ref[...], (((1,), (1,)), ((), ())),
                                         preferred_element_type=F32)

    def weighted_activations(s):
        for k in range(rows_per_sub):
            r = s * rows_per_sub + k
            rr = slice(k * PEER_NKEYS, (k + 1) * PEER_NKEYS)
            w = None
            for h in range(PEER_HEADS):
                wh = jnp.where(rj_ref[h] < all_rows(ni_ref[h, r:r + 1, :]),
                               all_rows(e0_ref[h, r:r + 1, :]) * e1_ref[h], zero)
                w = wh if w is None else w + wh
            a = a_refs[s][rr, :].astype(BF16)
            act = 0.5 * a * (1.0 + lax.erf(a * (2.0 ** -0.5)))
            p_refs[s][rr, :] = w * act

    pre_activations(0)
    for s in range(nsub):
        if s + 1 < nsub:
            pre_activations(s + 1)
        weighted_activations(s)
        rs = slice(s * PEER_SUB, (s + 1) * PEER_SUB)
        acc_ref[...] += jnp.dot(vt_ref[:, rs], p_refs[s][...], preferred_element_type=F32)

    @pl.when(e == pl.num_programs(1) - 1)
    def _():
        y = _rmsnorm_rows(x_ref[...] + jnp.transpose(acc_ref[...]), gf_ref[...])

        @pl.when(i_tok < np_tiles)
        def _():
            yp_ref[...] = y

        @pl.when(i_tok >= np_tiles)
        def _():
            ys_ref[...] = y


def peer_dense(xn, u, v_t, e0, ni, e1, rj, x, g_final, n_p, *, tt, te):
    n = xn.shape[0]
    np_tiles = n_p // tt
    nsub = te // PEER_SUB
    fspec = pl.BlockSpec((PEER_HEADS, PEER_NKEYS, tt), lambda i, e: (0, 0, i))
    rspec = pl.BlockSpec((PEER_HEADS, te // PEER_NKEYS, tt), lambda i, e: (0, e, i))
    return pl.pallas_call(
        functools.partial(_peer_dense_kernel, te=te, np_tiles=np_tiles),
        grid=(n // tt, PEER_EXPERTS // te),
        in_specs=[pl.BlockSpec((tt, D_MODEL), lambda i, e: (i, 0)),
                  pl.BlockSpec((te, D_MODEL), lambda i, e: (e, 0)),
                  pl.BlockSpec((D_MODEL, te), lambda i, e: (0, e)),
                  rspec, rspec, fspec, fspec,
                  pl.BlockSpec((tt, D_MODEL), lambda i, e: (i, 0)),
                  pl.BlockSpec((1, D_MODEL), lambda i, e: (0, 0))],
        out_specs=[pl.BlockSpec((tt, D_MODEL), lambda i, e: (jnp.minimum(i, np_tiles - 1), 0)),
                   pl.BlockSpec((tt, D_MODEL), lambda i, e: (jnp.maximum(i - np_tiles, 0), 0))],
        out_shape=[jax.ShapeDtypeStruct((n_p, D_MODEL), F32),
                   jax.ShapeDtypeStruct((n - n_p, D_MODEL), F32)],
        scratch_shapes=([pltpu.VMEM((D_MODEL, tt), F32)]
                        + [pltpu.VMEM((PEER_SUB, tt), F32)] * nsub
                        + [pltpu.VMEM((PEER_SUB, tt), BF16)] * nsub),
        compiler_params=_cparams(("arbitrary", "arbitrary")),
        name="peer_dense",
    )(xn, u, v_t, e0, ni, e1, rj, x, g_final.reshape(1, D_MODEL))


def _lane_row(vec, start, width=GATE_W):
    return jnp.zeros((1, width), F32).at[0, start:start + vec.shape[0]].set(vec.astype(F32))


def _rep_matrix(first_row, heads, lanes_per_head, rows=GATE_W):
    r = jnp.arange(rows)[:, None]
    c = jnp.arange(heads * lanes_per_head)[None, :]
    return ((c // lanes_per_head) + first_row == r).astype(BF16)


def _pad_tail(buf):
    return jnp.pad(buf, ((0, 0), (TAIL - (CONV_K - 1), 0), (0, 0)))


def kernel(x_prompt, x_sample, state_dn_conv, state_dn, state_ssm_conv, state_ssm, cache_mem_k, cache_mem_v,
           mem_prompt, g_mix, w_in, dn_conv_w, dn_A_log, dn_dt_bias, dn_norm_w, ssm_conv_w, ssm_conv_b,
           ssm_A_log, ssm_dt_bias, ssm_D, ssm_norm_w, w_out, g_xattn, g_mem, w_xq, w_mkv, w_xo, g_ffn, w_pq,
           peer_sub_keys, peer_u, peer_v, g_final):
    depth = g_mix.shape[0]
    assert depth == 1
    bp, lp, d = x_prompt.shape
    bs, ls, _ = x_sample.shape
    n_p, n_s = bp * lp, bs * ls
    n = n_p + n_s
    ncp = lp // CHUNK
    tm = 512
    x = (x_prompt.reshape(n_p, d), x_sample.reshape(n_s, d))

    wi = w_in[0]
    o_xbc = DN_QKV
    o_zdn = o_xbc + SSM_XBC
    o_zssm = o_zdn + DN_WIDTH
    o_small = o_zssm + SSM_WIDTH
    n_small = 2 * DN_HEADS + SSM_HEADS
    w_cat = jnp.concatenate([
        wi[:, :DN_QKV], wi[:, o_xbc:o_xbc + SSM_XBC],
        wi[:, o_small:o_small + n_small], jnp.zeros((d, COL_ZDN - COL_GATE - n_small), F32),
        wi[:, o_zdn:o_zdn + DN_WIDTH], wi[:, o_zssm:o_zssm + SSM_WIDTH]], axis=1).astype(BF16)

    proj = norm_matmul(x, g_mix[0], w_cat, tm=2 * tm, tn=1024)
    proj_c = proj.reshape(n // CHUNK, CHUNK, PROJ_COLS)
    proj_s = proj.reshape(n // ls, ls, PROJ_COLS)

    gdn_params = (dn_conv_w[0],
                  jnp.concatenate([_lane_row(dn_A_log[0], DN_HEADS), _lane_row(dn_dt_bias[0], DN_HEADS)], axis=0),
                  dn_norm_w[0].reshape(1, DN_DK),
                  _rep_matrix(0, DN_HEADS, DN_DK), _rep_matrix(DN_HEADS, DN_HEADS, DN_DK))
    ssd_params = (ssm_conv_w[0], ssm_conv_b[0].reshape(1, SSM_XBC),
                  jnp.concatenate([_lane_row(ssm_A_log[0], 2 * DN_HEADS), _lane_row(ssm_dt_bias[0], 2 * DN_HEADS)],
                                  axis=0),
                  jnp.repeat(ssm_D[0], SSM_HEADDIM).reshape(1, SSM_WIDTH),
                  ssm_norm_w[0].reshape(1, SSM_WIDTH),
                  _rep_matrix(2 * DN_HEADS, SSM_HEADS, SSM_HEADDIM), _rep_matrix(2 * DN_HEADS, SSM_HEADS, LANES))

    odn_p, p_dn = gdn_mixer(proj_c, 0, bp, ncp, CHUNK, 4, 1, gdn_params)
    odn_s, s_dn = gdn_mixer(proj_s, n_p // ls, bs, 1, ls, 1, 8, gdn_params,
                            buf=_pad_tail(state_dn_conv[0]), s0=state_dn[0])
    ossm_p, p_ss = ssd_mixer(proj_c, 0, bp, ncp, CHUNK, 4, 1, ssd_params)
    ossm_s, s_ss = ssd_mixer(proj_s, n_p // ls, bs, 1, ls, 1, 8, ssd_params,
                             buf=_pad_tail(state_ssm_conv[0]),
                             s0=state_ssm[0].reshape(bs, SSM_PAIRS, PAIR_W, SSM_STATE))
    wo = w_out[0].astype(BF16)
    x1 = mixer_out((odn_p.reshape(n_p, DN_WIDTH), odn_s.reshape(n_s, DN_WIDTH)),
                   (ossm_p.reshape(n_p, SSM_WIDTH), ossm_s.reshape(n_s, SSM_WIDTH)),
                   wo[:DN_WIDTH], wo[DN_WIDTH:], x, tm=tm)

    last_c = slice(ncp - 1, bp * ncp, ncp)
    hist_c = slice(CHUNK - (CONV_K - 1), CHUNK)
    hist_s = slice(ls - (CONV_K - 1), ls)
    p_dnc = proj_c[last_c, hist_c, COL_QKV:COL_QKV + DN_QKV]
    p_sc = proj_c[last_c, hist_c, COL_XBC:COL_XBC + SSM_XBC]
    s_dnc = proj_s[n_p // ls:, hist_s, COL_QKV:COL_QKV + DN_QKV]
    s_sc = proj_s[n_p // ls:, hist_s, COL_XBC:COL_XBC + SSM_XBC]

    mkv = norm_matmul(mem_prompt.reshape(bp * MEM_LEN, d), g_mem[0], w_mkv[0].astype(BF16), tm=tm, tn=1024)
    mkv3 = mkv.reshape(bp, MEM_LEN, 2 * d)
    q = norm_matmul(x1, g_xattn[0], w_xq[0].astype(BF16), tm=tm, tn=1024)
    tl = 512
    a_p = xattn_core(q.reshape(n // tl, tl, d), 0, bp, lp // tl, tl, mkv3, mkv3, kv_col=(0, 1))
    a_s = xattn_cache(q.reshape(n // ls, ls, d), n_p // ls, bs, ls, 4, cache_mem_k, cache_mem_v)
    x2 = matmul_res((a_p.reshape(n_p, d), a_s.reshape(n_s, d)), w_xo[0].astype(BF16), x1, tm=tm)

    xn, e0, ni, e1, rj = peer_route(x2, g_ffn[0], jnp.transpose(w_pq[0]).astype(BF16),
                                    peer_sub_keys[0].astype(BF16), tt=256)
    y_p, y_s = peer_dense(xn, peer_u[0].astype(BF16), jnp.transpose(peer_v[0]).astype(BF16),
                          e0, ni, e1, rj, x2, g_final, n_p, tt=512, te=2048)

    def heads(m):
        return m.reshape(bp, MEM_LEN, MEM_HEADS, MEM_HD)[None]

    return (y_p.reshape(bp, lp, d), y_s.reshape(bs, ls, d),
            p_dnc[None], p_dn[None], p_sc[None], p_ss.reshape(bp, SSM_HEADS, SSM_HEADDIM, SSM_STATE)[None],
            heads(mkv3[:, :, :d]), heads(mkv3[:, :, d:]),
            s_dnc[None], s_dn[None], s_sc[None], s_ss.reshape(bs, SSM_HEADS, SSM_HEADDIM, SSM_STATE)[None])
```

```python
import functools

import jax
import jax.numpy as jnp
from jax import lax
from jax.experimental import pallas as pl
from jax.experimental.pallas import tpu as pltpu

F32 = jnp.float32
BF16 = jnp.bfloat16
EPS = 1e-6

D_MODEL = 1024
CONV_K = 4
CHUNK = 64
DN_HEADS = 8
DN_DK = 128
DN_WIDTH = 1024
DN_QKV = 3072
SSM_HEADS = 16
SSM_HEADDIM = 64
SSM_STATE = 128
SSM_WIDTH = 1024
SSM_XBC = 1536
MEM_LEN = 256
MEM_HEADS = 4
MEM_HD = 256
PEER_HEADS = 8
PEER_NKEYS = 128
PEER_EXPERTS = PEER_NKEYS * PEER_NKEYS
PEER_TOPK = 16
PEER_HALF = 128

COL_QKV = 0
COL_XBC = 3072
COL_GATE = 4608
COL_ZDN = 5120
COL_ZSSM = 6144
PROJ_COLS = 7168
GATE_W = 128
TAIL = 8
INV_BLOCK = 16
NOT_TOP = 99.0
LANES = 128
BF16_SUBLANES = 16
PEER_SUB = 1024

VMEM_LIMIT = 56 * 1024 * 1024


def _cparams(sem):
    return pltpu.CompilerParams(dimension_semantics=sem, vmem_limit_bytes=VMEM_LIMIT)


def _bdot(a, b):
    return jnp.dot(a.astype(BF16), b.astype(BF16), preferred_element_type=F32)


def _bdot_nt(a, b):
    return lax.dot_general(a.astype(BF16), b.astype(BF16), (((1,), (1,)), ((), ())),
                           preferred_element_type=F32)


def _bdot_tn(a, b):
    return lax.dot_general(a.astype(BF16), b.astype(BF16), (((0,), (0,)), ((), ())),
                           preferred_element_type=F32)


def _split3(x):
    hi = x.astype(BF16)
    r = x - hi.astype(F32)
    mid = r.astype(BF16)
    lo = (r - mid.astype(F32)).astype(BF16)
    return hi, mid, lo


def _dot_sel_rhs(x, sel):
    hi, mid, lo = _split3(x)
    d = functools.partial(jnp.dot, preferred_element_type=F32)
    return d(hi, sel) + d(mid, sel) + d(lo, sel)


def _dot_sel_lhs(sel, x):
    hi, mid, lo = _split3(x)
    d = functools.partial(jnp.dot, preferred_element_type=F32)
    return d(sel, hi) + d(sel, mid) + d(sel, lo)


def _silu(x):
    return x * jax.nn.sigmoid(x)


def _softplus(x):
    return jnp.maximum(x, 0.0) + jnp.log1p(jnp.exp(-jnp.abs(x)))


def _unit_lower_inverse(lms, n):
    row = lax.broadcasted_iota(jnp.int32, (n, n), 0)
    col = lax.broadcasted_iota(jnp.int32, (n, n), 1)
    eye = (row == col).astype(F32)

    def nilpotent_inverse(xs, index):
        invs = [eye - x for x in xs]
        ps = xs
        k = 2
        while k < index:
            ps = [_bdot(p, p) for p in ps]
            invs = [inv + _bdot(inv, p) for inv, p in zip(invs, ps)]
            k *= 2
        return invs

    if n <= INV_BLOCK:
        return nilpotent_inverse(lms, n)
    shift = INV_BLOCK.bit_length() - 1
    same = jnp.right_shift(row, shift) == jnp.right_shift(col, shift)
    dinvs = nilpotent_inverse([jnp.where(same, lm, 0.0) for lm in lms], INV_BLOCK)
    fs = [_bdot(dinv, jnp.where(same, 0.0, lm)) for dinv, lm in zip(dinvs, lms)]
    finvs = nilpotent_inverse(fs, n // INV_BLOCK)
    return [_bdot(finv, dinv) for finv, dinv in zip(finvs, dinvs)]


def _causal_conv(x, xp_ref, w, lc):
    xp_ref[TAIL:TAIL + lc, :] = x
    y = x * w[3:4, :]
    for s in range(1, CONV_K):
        y = y + xp_ref[TAIL - s:TAIL - s + lc, :] * w[CONV_K - 1 - s:CONV_K - s, :]
    xp_ref[0:TAIL, :] = xp_ref[lc:lc + TAIL, :]
    return y


def _decay_matrix(col, row, lower_incl):
    diff = jnp.where(lower_incl, col - row, 0.0)
    return jnp.where(lower_incl, jnp.exp(diff), 0.0)


def _pair_specs(tm, k, np_tiles, two_axes):
    if two_axes:
        return [pl.BlockSpec((tm, k), lambda i, j: (jnp.minimum(i, np_tiles - 1), 0)),
                pl.BlockSpec((tm, k), lambda i, j: (jnp.maximum(i - np_tiles, 0), 0))]
    return [pl.BlockSpec((tm, k), lambda i: (jnp.minimum(i, np_tiles - 1), 0)),
            pl.BlockSpec((tm, k), lambda i: (jnp.maximum(i - np_tiles, 0), 0))]


def _pick(is_prompt, p_ref, s_ref):
    return jnp.where(is_prompt, p_ref[...], s_ref[...])


def _rmsnorm_rows(x, g):
    ms = jnp.mean(x * x, axis=-1, keepdims=True)
    return x * lax.rsqrt(ms + EPS) * g


def _norm_matmul_kernel(*refs, np_tiles):
    if np_tiles is None:
        x_ref, g_ref, w_ref, o_ref, xn_ref = refs
    else:
        xp_ref, xs_ref, g_ref, w_ref, o_ref, xn_ref = refs
        is_prompt = pl.program_id(0) < np_tiles

    @pl.when(pl.program_id(1) == 0)
    def _():
        x = x_ref[...] if np_tiles is None else _pick(is_prompt, xp_ref, xs_ref)
        xn_ref[...] = _rmsnorm_rows(x, g_ref[...]).astype(BF16)

    o_ref[...] = jnp.dot(xn_ref[...], w_ref[...], preferred_element_type=F32).astype(o_ref.dtype)


def norm_matmul(x, g, w, *, tm, tn, out_dtype=F32):
    pair = isinstance(x, tuple)
    n = sum(a.shape[0] for a in x) if pair else x.shape[0]
    k, m = w.shape
    assert n % tm == 0 and m % tn == 0
    if pair:
        assert x[0].shape[0] % tm == 0
        np_tiles = x[0].shape[0] // tm
        x_specs = _pair_specs(tm, k, np_tiles, True)
        x_args = list(x)
    else:
        np_tiles = None
        x_specs = [pl.BlockSpec((tm, k), lambda i, j: (i, 0))]
        x_args = [x]
    return pl.pallas_call(
        functools.partial(_norm_matmul_kernel, np_tiles=np_tiles),
        grid=(n // tm, m // tn),
        in_specs=x_specs + [pl.BlockSpec((1, k), lambda i, j: (0, 0)),
                            pl.BlockSpec((k, tn), lambda i, j: (0, j))],
        out_specs=pl.BlockSpec((tm, tn), lambda i, j: (i, j)),
        out_shape=jax.ShapeDtypeStruct((n, m), out_dtype),
        scratch_shapes=[pltpu.VMEM((tm, k), BF16)],
        compiler_params=_cparams(("arbitrary", "arbitrary")),
        name="norm_matmul",
    )(*x_args, g.reshape(1, k), w)


def _mixer_out_kernel(dp_ref, ds_ref, sp_ref, ss_ref, w1_ref, w2_ref, xp_ref, xs_ref, o_ref, *, np_tiles):
    is_prompt = pl.program_id(0) < np_tiles
    a1 = _pick(is_prompt, dp_ref, ds_ref).astype(BF16)
    a2 = _pick(is_prompt, sp_ref, ss_ref).astype(BF16)
    acc = jnp.dot(a1, w1_ref[...], preferred_element_type=F32)
    acc = acc + jnp.dot(a2, w2_ref[...], preferred_element_type=F32)
    o_ref[...] = _pick(is_prompt, xp_ref, xs_ref) + acc


def mixer_out(o_dn, o_ssm, w1, w2, x, *, tm):
    n = x[0].shape[0] + x[1].shape[0]
    k = w1.shape[0]
    m = w1.shape[1]
    np_tiles = x[0].shape[0] // tm
    wspec = pl.BlockSpec((k, m), lambda i: (0, 0))
    return pl.pallas_call(
        functools.partial(_mixer_out_kernel, np_tiles=np_tiles),
        grid=(n // tm,),
        in_specs=(_pair_specs(tm, k, np_tiles, False) + _pair_specs(tm, k, np_tiles, False)
                  + [wspec, wspec] + _pair_specs(tm, m, np_tiles, False)),
        out_specs=pl.BlockSpec((tm, m), lambda i: (i, 0)),
        out_shape=jax.ShapeDtypeStruct((n, m), F32),
        compiler_params=_cparams(("arbitrary",)),
        name="mixer_out",
    )(*o_dn, *o_ssm, w1, w2, *x)


def _matmul_res_kernel(ap_ref, as_ref, w_ref, r_ref, o_ref, *, np_tiles):
    a = _pick(pl.program_id(0) < np_tiles, ap_ref, as_ref).astype(BF16)
    o_ref[...] = r_ref[...] + jnp.dot(a, w_ref[...], preferred_element_type=F32)


def matmul_res(a, w, res, *, tm):
    n, m = res.shape
    k = w.shape[0]
    np_tiles = a[0].shape[0] // tm
    return pl.pallas_call(
        functools.partial(_matmul_res_kernel, np_tiles=np_tiles),
        grid=(n // tm,),
        in_specs=_pair_specs(tm, k, np_tiles, False) + [pl.BlockSpec((k, m), lambda i: (0, 0)),
                                                        pl.BlockSpec((tm, m), lambda i: (i, 0))],
        out_specs=pl.BlockSpec((tm, m), lambda i: (i, 0)),
        out_shape=jax.ShapeDtypeStruct((n, m), F32),
        compiler_params=_cparams(("arbitrary",)),
        name="matmul_res",
    )(*a, w, res)


def _gdn_kernel(*refs, lc, ng, nbk, has_state):
    refs = list(refs)

    def take(k):
        out = refs[:k]
        del refs[:k]
        return out

    qkv_refs, gate_refs, z_refs = take(ng), take(ng), take(ng)
    if has_state:
        buf_ref, s0_ref = take(2)
    cw_ref, gp_ref, nw_ref, repb_ref, repg_ref, o_ref, s_ref, xp_ref = take(8)
    nseq = ng * nbk
    where = [divmod(j, nbk) for j in range(nseq)]

    @pl.when(pl.program_id(1) == 0)
    def _():
        if has_state:
            xp_ref[:, 0:TAIL, :] = buf_ref[...]
            s_ref[...] = s0_ref[...]
        else:
            xp_ref[:, 0:TAIL, :] = jnp.zeros((nseq, TAIL, DN_QKV), F32)
            s_ref[...] = jnp.zeros(s_ref.shape, F32)

    row = lax.broadcasted_iota(jnp.int32, (lc, lc), 0)
    col = lax.broadcasted_iota(jnp.int32, (lc, lc), 1)
    incl = row >= col
    strict = row > col
    tri = incl.astype(BF16)

    seqs = []
    for j, (g_, k_) in enumerate(where):
        qkv = _silu(_causal_conv(qkv_refs[g_][k_], xp_ref.at[j], cw_ref[...], lc))
        gate = gate_refs[g_][k_]
        beta_all = jax.nn.sigmoid(gate)
        g_all = -jnp.exp(gp_ref[0:1, :]) * _softplus(gate + gp_ref[1:2, :])
        gc = _dot_sel_lhs(tri, g_all)
        gc_b = _dot_sel_rhs(gc, repg_ref[...])
        gc_last_b = gc_b[lc - 1:lc, :]
        seqs.append(dict(qkv=qkv, gc_t=jnp.transpose(gc), gc_b=gc_b,
                         beta_b=_dot_sel_rhs(beta_all, repb_ref[...]),
                         egc_b=jnp.exp(gc_b), ekd_b=jnp.exp(gc_last_b - gc_b), egt_b=jnp.exp(gc_last_b)))

    units = [(j, h) for j in range(nseq) for h in range(DN_HEADS)]
    qs, ks, vbs, kbegs, decays = [], [], [], [], []
    for j, h in units:
        sq = seqs[j]
        sl = slice(h * DN_DK, (h + 1) * DN_DK)
        qh = sq["qkv"][:, h * DN_DK:(h + 1) * DN_DK]
        kh = sq["qkv"][:, DN_WIDTH + h * DN_DK:DN_WIDTH + (h + 1) * DN_DK]
        vh = sq["qkv"][:, 2 * DN_WIDTH + h * DN_DK:2 * DN_WIDTH + (h + 1) * DN_DK]
        qh = qh * lax.rsqrt(jnp.sum(qh * qh, axis=-1, keepdims=True) + EPS) * (DN_DK ** -0.5)
        kh = kh * lax.rsqrt(jnp.sum(kh * kh, axis=-1, keepdims=True) + EPS)
        bh = sq["beta_b"][:, sl]
        qs.append(qh)
        ks.append(kh)
        vbs.append(vh * bh)
        kbegs.append((kh * bh, sq["egc_b"][:, sl]))
        decays.append(_decay_matrix(sq["gc_b"][:, h * DN_DK:h * DN_DK + lc],
                                    sq["gc_t"][DN_HEADS + h:DN_HEADS + h + 1, :], incl))
    lowers = [jnp.where(strict, _bdot_nt(kb, kh) * dec, 0.0)
              for (kb, _), kh, dec in zip(kbegs, ks, decays)]
    attns = [_bdot_nt(qh, kh) * dec for qh, kh, dec in zip(qs, ks, decays)]
    tinvs = _unit_lower_inverse(lowers, lc)
    us = [_bdot(t, vb) for t, vb in zip(tinvs, vbs)]
    ws = [_bdot(t, kb * eg) for t, (kb, eg) in zip(tinvs, kbegs)]
    states = [s_ref[j, h] for j, h in units]
    v_news = [u - _bdot(w, s) for u, w, s in zip(us, ws, states)]
    os_ = [_bdot(qh * eg, s) + _bdot(a, vn)
           for qh, (_, eg), s, a, vn in zip(qs, kbegs, states, attns, v_news)]
    for (j, h), kh, s, vn, o in zip(units, ks, states, v_news, os_):
        g_, k_ = where[j]
        sl = slice(h * DN_DK, (h + 1) * DN_DK)
        sq = seqs[j]
        s_ref[j, h] = s * sq["egt_b"][:, sl] + _bdot_tn(kh * sq["ekd_b"][:, sl], vn)
        o = o * lax.rsqrt(jnp.mean(o * o, axis=-1, keepdims=True) + EPS) * nw_ref[...]
        o_ref[j, 0, :, sl] = o * _silu(z_refs[g_][k_, :, sl])


def gdn_mixer(proj3, row0, nseq, nchunk, lc, ng, nbk, params, buf=None, s0=None):
    has_state = buf is not None
    step = ng * nbk
    assert nseq % step == 0 and row0 % nbk == 0 and (nbk == 1 or nchunk == 1)

    def proj_specs(width, col_block):
        return [pl.BlockSpec((nbk, lc, width),
                             lambda b, c, g_=g_: (row0 // nbk + (b * ng + g_) * nchunk + c, 0, col_block))
                for g_ in range(ng)]

    in_specs = (proj_specs(DN_QKV, COL_QKV // DN_QKV) + proj_specs(GATE_W, COL_GATE // GATE_W)
                + proj_specs(DN_WIDTH, COL_ZDN // DN_WIDTH))
    args = [proj3] * (3 * ng)
    state_spec = pl.BlockSpec((step, DN_HEADS, DN_DK, DN_DK), lambda b, c: (b, 0, 0, 0))
    if has_state:
        in_specs += [pl.BlockSpec((step, TAIL, DN_QKV), lambda b, c: (b, 0, 0)), state_spec]
        args += [buf, s0]
    for prm in params:
        in_specs.append(pl.BlockSpec(prm.shape, lambda b, c: (0, 0)))
        args.append(prm)
    return pl.pallas_call(
        functools.partial(_gdn_kernel, lc=lc, ng=ng, nbk=nbk, has_state=has_state),
        grid=(nseq // step, nchunk),
        in_specs=in_specs,
        out_specs=[pl.BlockSpec((step, 1, lc, DN_WIDTH), lambda b, c: (b, c, 0, 0)), state_spec],
        out_shape=[jax.ShapeDtypeStruct((nseq, nchunk, lc, DN_WIDTH), F32),
                   jax.ShapeDtypeStruct((nseq, DN_HEADS, DN_DK, DN_DK), F32)],
        scratch_shapes=[pltpu.VMEM((step, TAIL + lc, DN_QKV), F32)],
        compiler_params=_cparams(("arbitrary", "arbitrary")),
        name="gdn_mixer_state" if has_state else "gdn_mixer",
    )(*args)


SSM_PAIRS = SSM_HEADS // 2
PAIR_W = 2 * SSM_HEADDIM
GROUP_W = SSM_WIDTH // 2


def _ssd_kernel(*refs, lc, ng, nbk, has_state):
    refs = list(refs)

    def take(k):
        out = refs[:k]
        del refs[:k]
        return out

    xbc_refs, gate_refs, z_refs = take(ng), take(ng), take(ng)
    if has_state:
        buf_ref, s0_ref = take(2)
    (cw_ref, cb_ref, gp_ref, dd_ref, nw_ref, reps_ref, repw_ref, o_ref, s_ref, xp_ref) = take(10)
    nseq = ng * nbk
    where = [divmod(j, nbk) for j in range(nseq)]

    @pl.when(pl.program_id(1) == 0)
    def _():
        if has_state:
            xp_ref[:, 0:TAIL, :] = buf_ref[...]
            s_ref[...] = s0_ref[...]
        else:
            xp_ref[:, 0:TAIL, :] = jnp.zeros((nseq, TAIL, SSM_XBC), F32)
            s_ref[...] = jnp.zeros(s_ref.shape, F32)

    row = lax.broadcasted_iota(jnp.int32, (lc, lc), 0)
    col = lax.broadcasted_iota(jnp.int32, (lc, lc), 1)
    incl = row >= col
    tri = incl.astype(BF16)
    first_head = lax.broadcasted_iota(jnp.int32, (lc, PAIR_W), 1) < SSM_HEADDIM

    seqs = []
    for j, (g_, k_) in enumerate(where):
        xbc = _silu(_causal_conv(xbc_refs[g_][k_], xp_ref.at[j], cw_ref[...], lc) + cb_ref[...])
        xs = xbc[:, :SSM_WIDTH]
        dt = _softplus(gate_refs[g_][k_] + gp_ref[1:2, :])
        acs = _dot_sel_lhs(tri, dt * (-jnp.exp(gp_ref[0:1, :])))
        acs_b = _dot_sel_rhs(acs, reps_ref[...])
        acs_w = _dot_sel_rhs(acs, repw_ref[...])
        seqs.append(dict(xbc=xbc, xs=xs, acs_t=jnp.transpose(acs), acs_w=acs_w,
                         xd=xs * _dot_sel_rhs(dt, reps_ref[...]),
                         eacs_b=jnp.exp(acs_b), ends_b=jnp.exp(acs_b[lc - 1:lc, :] - acs_b),
                         ecd_w=jnp.exp(acs_w[lc - 1:lc, :])))

    def bmat(sq, g):
        return sq["xbc"][:, SSM_WIDTH + g * SSM_STATE:SSM_WIDTH + (g + 1) * SSM_STATE]

    def cmat(sq, g):
        o = SSM_WIDTH + 2 * SSM_STATE
        return sq["xbc"][:, o + g * SSM_STATE:o + (g + 1) * SSM_STATE]

    cbs = [[_bdot_nt(cmat(sq, g), bmat(sq, g)) for g in range(2)] for sq in seqs]
    units = [(j, p) for j in range(nseq) for p in range(SSM_PAIRS)]
    lmats = [[_decay_matrix(seqs[j]["acs_w"][:, hh * LANES:hh * LANES + lc],
                            seqs[j]["acs_t"][2 * DN_HEADS + hh:2 * DN_HEADS + hh + 1, :], incl)
              for hh in (2 * p, 2 * p + 1)] for j, p in units]
    prevs = [s_ref[j, p] for j, p in units]
    y_diag, y_off, sts = [], [], []
    for (j, p), lm, prev in zip(units, lmats, prevs):
        sq = seqs[j]
        g = p // (SSM_PAIRS // 2)
        psl = slice(p * PAIR_W, (p + 1) * PAIR_W)
        xd_p = sq["xd"][:, psl]
        y_diag.append([_bdot(cbs[j][g] * lm[0], xd_p), _bdot(cbs[j][g] * lm[1], xd_p)])
        y_off.append(_bdot_nt(cmat(sq, g), prev))
        sts.append(_bdot_tn(xd_p * sq["ends_b"][:, psl], bmat(sq, g)))
    ys = []
    ssq = [[None, None] for _ in range(nseq)]
    for (j, p), yd, yo, st, prev in zip(units, y_diag, y_off, sts, prevs):
        sq = seqs[j]
        g_, k_ = where[j]
        g = p // (SSM_PAIRS // 2)
        psl = slice(p * PAIR_W, (p + 1) * PAIR_W)
        cd = jnp.concatenate(
            [jnp.broadcast_to(sq["ecd_w"][:, (2 * p) * LANES:(2 * p + 1) * LANES], (SSM_HEADDIM, SSM_STATE)),
             jnp.broadcast_to(sq["ecd_w"][:, (2 * p + 1) * LANES:(2 * p + 2) * LANES], (SSM_HEADDIM, SSM_STATE))],
            axis=0)
        s_ref[j, p] = prev * cd + st
        y = jnp.where(first_head, yd[0], yd[1]) + yo * sq["eacs_b"][:, psl] + dd_ref[:, psl] * sq["xs"][:, psl]
        y = y * _silu(z_refs[g_][k_, :, psl])
        s2 = jnp.sum(y * y, axis=-1, keepdims=True)
        ssq[j][g] = s2 if ssq[j][g] is None else ssq[j][g] + s2
        ys.append(y)
    for (j, p), y in zip(units, ys):
        g = p // (SSM_PAIRS // 2)
        psl = slice(p * PAIR_W, (p + 1) * PAIR_W)
        o_ref[j, 0, :, psl] = y * lax.rsqrt(ssq[j][g] * (1.0 / GROUP_W) + EPS) * nw_ref[:, psl]


def ssd_mixer(proj3, row0, nseq, nchunk, lc, ng, nbk, params, buf=None, s0=None):
    has_state = buf is not None
    step = ng * nbk
    assert nseq % step == 0 and row0 % nbk == 0 and (nbk == 1 or nchunk == 1)

    def proj_specs(width, col_block):
        return [pl.BlockSpec((nbk, lc, width),
                             lambda b, c, g_=g_: (row0 // nbk + (b * ng + g_) * nchunk + c, 0, col_block))
                for g_ in range(ng)]

    in_specs = (proj_specs(SSM_XBC, COL_XBC // SSM_XBC) + proj_specs(GATE_W, COL_GATE // GATE_W)
                + proj_specs(SSM_WIDTH, COL_ZSSM // SSM_WIDTH))
    args = [proj3] * (3 * ng)
    state_spec = pl.BlockSpec((step, SSM_PAIRS, PAIR_W, SSM_STATE), lambda b, c: (b, 0, 0, 0))
    if has_state:
        in_specs += [pl.BlockSpec((step, TAIL, SSM_XBC), lambda b, c: (b, 0, 0)), state_spec]
        args += [buf, s0]
    for prm in params:
        in_specs.append(pl.BlockSpec(prm.shape, lambda b, c: (0, 0)))
        args.append(prm)
    return pl.pallas_call(
        functools.partial(_ssd_kernel, lc=lc, ng=ng, nbk=nbk, has_state=has_state),
        grid=(nseq // step, nchunk),
        in_specs=in_specs,
        out_specs=[pl.BlockSpec((step, 1, lc, SSM_WIDTH), lambda b, c: (b, c, 0, 0)), state_spec],
        out_shape=[jax.ShapeDtypeStruct((nseq, nchunk, lc, SSM_WIDTH), F32),
                   jax.ShapeDtypeStruct((nseq, SSM_PAIRS, PAIR_W, SSM_STATE), F32)],
        scratch_shapes=[pltpu.VMEM((step, TAIL + lc, SSM_XBC), F32)],
        compiler_params=_cparams(("arbitrary", "arbitrary")),
        name="ssd_mixer_state" if has_state else "ssd_mixer",
    )(*args)


def _xattn_kernel(q_ref, k_ref, v_ref, o_ref):
    for h in range(MEM_HEADS):
        sl = slice(h * MEM_HD, (h + 1) * MEM_HD)
        k = k_ref[0, :, sl]
        v = v_ref[0, :, sl]
        s = _bdot_nt(q_ref[0, :, sl], k) * (MEM_HD ** -0.5)
        s = s - jnp.max(s, axis=-1, keepdims=True)
        p = jnp.exp(s)
        p = p / jnp.sum(p, axis=-1, keepdims=True)
        o_ref[0, :, sl] = _bdot(p, v)


def _xattn_cache_kernel(q_ref, k_ref, v_ref, o_ref, *, tl, nb):
    shape = (MEM_HEADS * tl, MEM_LEN * MEM_HEADS)
    row_head = jnp.right_shift(lax.broadcasted_iota(jnp.int32, shape, 0), tl.bit_length() - 1)
    col_head = jnp.bitwise_and(lax.broadcasted_iota(jnp.int32, shape, 1), MEM_HEADS - 1)
    own_head = row_head == col_head
    for j in range(nb):
        k2 = k_ref[j].reshape(MEM_LEN * MEM_HEADS, MEM_HD)
        v2 = v_ref[j].reshape(MEM_LEN * MEM_HEADS, MEM_HD)
        q4 = jnp.concatenate([q_ref[j, :, h * MEM_HD:(h + 1) * MEM_HD] for h in range(MEM_HEADS)], axis=0)
        s = jnp.where(own_head, _bdot_nt(q4, k2) * (MEM_HD ** -0.5), -jnp.inf)
        s = s - jnp.max(s, axis=-1, keepdims=True)
        p = jnp.exp(s)
        p = p / jnp.sum(p, axis=-1, keepdims=True)
        o4 = _bdot(p, v2)
        for h in range(MEM_HEADS):
            o_ref[j, :, h * MEM_HD:(h + 1) * MEM_HD] = o4[h * tl:(h + 1) * tl, :]


def xattn_cache(q3, row0, nseq, tl, nb, cache_k, cache_v):
    assert tl & (tl - 1) == 0 and MEM_HEADS & (MEM_HEADS - 1) == 0 and nseq % nb == 0 and row0 % nb == 0
    kv_spec = pl.BlockSpec((None, nb, MEM_LEN, MEM_HEADS, MEM_HD), lambda b: (0, b, 0, 0, 0))
    return pl.pallas_call(
        functools.partial(_xattn_cache_kernel, tl=tl, nb=nb),
        grid=(nseq // nb,),
        in_specs=[pl.BlockSpec((nb, tl, D_MODEL), lambda b: (row0 // nb + b, 0, 0)), kv_spec, kv_spec],
        out_specs=pl.BlockSpec((nb, tl, D_MODEL), lambda b: (b, 0, 0)),
        out_shape=jax.ShapeDtypeStruct((nseq, tl, D_MODEL), F32),
        compiler_params=_cparams(("arbitrary",)),
        name="xattn_cache",
    )(q3, cache_k, cache_v)


def xattn_core(q3, row0, nseq, ntile, tl, mem_k, mem_v, kv_col=(0, 0)):
    kv_specs = [pl.BlockSpec((1, MEM_LEN, D_MODEL), lambda b, t, c_=c_: (b, 0, c_)) for c_ in kv_col]
    return pl.pallas_call(
        _xattn_kernel,
        grid=(nseq, ntile),
        in_specs=[pl.BlockSpec((1, tl, D_MODEL), lambda b, t: (row0 + b * ntile + t, 0, 0))] + kv_specs,
        out_specs=pl.BlockSpec((1, tl, D_MODEL), lambda b, t: (b * ntile + t, 0, 0)),
        out_shape=jax.ShapeDtypeStruct((nseq * ntile, tl, D_MODEL), F32),
        compiler_params=_cparams(("arbitrary", "arbitrary")),
        name="xattn_core",
    )(q3, mem_k, mem_v)


MAGNITUDE_BITS = 0x7FFFFFFF
NEG_INF_KEY = -0x800000 ^ MAGNITUDE_BITS
RANK0_MARK = NEG_INF_KEY - 1


def _order_key(x, inverse=False):
    b = x if inverse else lax.bitcast_convert_type(x, jnp.int32)
    k = jnp.where(b < 0, jnp.bitwise_xor(b, MAGNITUDE_BITS), b)
    return lax.bitcast_convert_type(k, F32) if inverse else k


def _top16_rows(s, break_ties):
    n = s.shape[0]
    v = _order_key(s + 0.0)
    vals = []
    for a in range(PEER_TOPK):
        m = jnp.max(v, axis=0, keepdims=True)
        hit = v == m
        if break_ties:
            iota = lax.broadcasted_iota(jnp.int32, s.shape, 0)
            hit = iota == jnp.min(jnp.where(hit, iota, n), axis=0, keepdims=True)
        v = jnp.where(hit, RANK0_MARK - a, v)
        vals.append(m)
    rank = jnp.where(v <= RANK0_MARK, (RANK0_MARK - v).astype(F32), NOT_TOP)
    return _order_key(jnp.concatenate(vals, axis=0), inverse=True), rank


def _pair_top16(s0v, s1v):
    t = s0v.shape[1]
    iota = lax.broadcasted_iota(jnp.int32, (PEER_TOPK, t), 0).astype(F32)
    n = jnp.zeros((PEER_TOPK, t), F32)
    front = s0v + s1v[0:1, :]
    top = s0v[0:1, :] + s1v[0:1, :]
    z = jnp.zeros((1, t), F32)
    for _ in range(PEER_TOPK):
        m = jnp.max(front, axis=0, keepdims=True)
        a_star = jnp.min(jnp.where(front == m, iota, float(PEER_TOPK)), axis=0, keepdims=True)
        hit = iota == a_star
        z = z + jnp.exp(m - top)
        n = jnp.where(hit, n + 1.0, n)
        nxt = jnp.full((PEER_TOPK, t), -jnp.inf, F32)
        for b in range(1, PEER_TOPK):
            nxt = jnp.where(n == float(b), s1v[b:b + 1, :], nxt)
        front = jnp.where(hit, s0v + nxt, front)
    return n, z


def _peer_route_kernel(x_ref, g_ref, wq_ref, sk_ref, xn_ref, e0_ref, ni_ref, e1_ref, rj_ref, q_ref):
    xn = _rmsnorm_rows(x_ref[...], g_ref[...]).astype(BF16)
    xn_ref[...] = xn
    q_ref[...] = lax.dot_general(wq_ref[...], xn, (((1,), (1,)), ((), ())),
                                 preferred_element_type=F32).astype(BF16)
    tt = x_ref.shape[0]

    def route(break_ties):
        bad = jnp.zeros((1, LANES), F32)
        for lb in range(tt // LANES):
            ls = slice(lb * LANES, (lb + 1) * LANES)
            for h in range(PEER_HEADS):
                def ranked_half(c):
                    r0 = (2 * h + c) * PEER_HALF
                    s = jnp.dot(sk_ref[c], q_ref[r0:r0 + PEER_HALF, ls],
                                preferred_element_type=F32)
                    vals, rank = _top16_rows(s, break_ties)
                    inside = rank < float(PEER_TOPK)
                    ex = jnp.where(inside, jnp.exp(jnp.where(inside, s - vals[0:1, :], 0.0)), 0.0)
                    count = jnp.sum(jnp.where(inside, 1.0, 0.0), axis=0, keepdims=True)
                    return vals, rank, ex, jnp.abs(count - float(PEER_TOPK))

                s1v, rank1, e1, off1 = ranked_half(1)
                e1_ref[h, :, ls] = e1.astype(BF16)
                rj_ref[h, :, ls] = rank1.astype(BF16)
                s0v, rank0, e0, off0 = ranked_half(0)
                if not break_ties:
                    bad = jnp.maximum(bad, jnp.maximum(off0, off1))
                n, z = _pair_top16(s0v, s1v)
                e0_ref[h, :, ls] = e0 / z
                ni = jnp.zeros_like(rank0)
                for a in range(PEER_TOPK):
                    ni = jnp.where(rank0 == float(a), n[a:a + 1, :], ni)
                ni_ref[h, :, ls] = ni
        return bad

    has_ties = jnp.max(route(break_ties=False)) > 0.0

    @pl.when(has_ties)
    def _():
        route(break_ties=True)


def peer_route(x, g, wq_t, sub_keys, *, tt):
    n = x.shape[0]
    fspec = pl.BlockSpec((None, PEER_HEADS, PEER_NKEYS, tt), lambda i: (i, 0, 0, 0))

    def fac(dtype):
        return jax.ShapeDtypeStruct((n // tt, PEER_HEADS, PEER_NKEYS, tt), dtype)

    return pl.pallas_call(
        _peer_route_kernel,
        grid=(n // tt,),
        in_specs=[pl.BlockSpec((tt, D_MODEL), lambda i: (i, 0)),
                  pl.BlockSpec((1, D_MODEL), lambda i: (0, 0)),
                  pl.BlockSpec(wq_t.shape, lambda i: (0, 0)),
                  pl.BlockSpec(sub_keys.shape, lambda i: (0, 0, 0))],
        out_specs=[pl.BlockSpec((tt, D_MODEL), lambda i: (i, 0)), fspec, fspec, fspec, fspec],
        out_shape=[jax.ShapeDtypeStruct((n, D_MODEL), BF16), fac(F32), fac(F32), fac(BF16), fac(BF16)],
        scratch_shapes=[pltpu.VMEM((PEER_HEADS * 2 * PEER_HALF, tt), BF16)],
        compiler_params=_cparams(("arbitrary",)),
        name="peer_route",
    )(x, g.reshape(1, D_MODEL), wq_t, sub_keys)


def _peer_dense_kernel(xn_ref, u_ref, vt_ref, e0_ref, ni_ref, e1_ref, rj_ref, x_ref, gf_ref,
                       yp_ref, ys_ref, acc_ref, *sub_refs, te, np_tiles):
    i_tok = pl.program_id(0)
    e = pl.program_id(1)
    nsub = te // PEER_SUB
    a_refs, p_refs = sub_refs[:nsub], sub_refs[nsub:]
    tt = xn_ref.shape[0]
    rows_per_sub = PEER_SUB // PEER_NKEYS
    zero = jnp.zeros((), BF16)

    @pl.when(e == 0)
    def _():
        acc_ref[...] = jnp.zeros_like(acc_ref)

    def all_rows(row):
        tile = jnp.broadcast_to(row, (BF16_SUBLANES, tt)).astype(BF16)
        return jnp.concatenate([tile] * (PEER_NKEYS // BF16_SUBLANES), axis=0)

    def pre_activations(s):
        rs = slice(s * PEER_SUB, (s + 1) * PEER_SUB)
        a_refs[s][...] = lax.dot_general(u_ref[rs, :], xn_ref[...], (((1,), (1,)), ((), ())),
                                         preferred_element_type=F32)

    def weighted_activations(s):
        for k in range(rows_per_sub):
            r = s * rows_per_sub + k
            rr = slice(k * PEER_NKEYS, (k + 1) * PEER_NKEYS)
            w = None
            for h in range(PEER_HEADS):
                wh = jnp.where(rj_ref[h] < all_rows(ni_ref[h, r:r + 1, :]),
                               all_rows(e0_ref[h, r:r + 1, :]) * e1_ref[h], zero)
                w = wh if w is None else w + wh
            a = a_refs[s][rr, :].astype(BF16)
            act = 0.5 * a * (1.0 + lax.erf(a * (2.0 ** -0.5)))
            p_refs[s][rr, :] = w * act

    pre_activations(0)
    for s in range(nsub):
        if s + 1 < nsub:
            pre_activations(s + 1)
        weighted_activations(s)
        rs = slice(s * PEER_SUB, (s + 1) * PEER_SUB)
        acc_ref[...] += jnp.dot(vt_ref[:, rs], p_refs[s][...], preferred_element_type=F32)

    @pl.when(e == pl.num_programs(1) - 1)
    def _():
        y = _rmsnorm_rows(x_ref[...] + jnp.transpose(acc_ref[...]), gf_ref[...])

        @pl.when(i_tok < np_tiles)
        def _():
            yp_ref[...] = y

        @pl.when(i_tok >= np_tiles)
        def _():
            ys_ref[...] = y


def peer_dense(xn, u, v_t, e0, ni, e1, rj, x, g_final, n_p, *, tt, te):
    n = xn.shape[0]
    np_tiles = n_p // tt
    nsub = te // PEER_SUB
    assert e0.shape == (n // tt, PEER_HEADS, PEER_NKEYS, tt)
    fspec = pl.BlockSpec((None, PEER_HEADS, PEER_NKEYS, tt), lambda i, e: (i, 0, 0, 0))
    rspec = pl.BlockSpec((None, PEER_HEADS, te // PEER_NKEYS, tt), lambda i, e: (i, 0, e, 0))
    return pl.pallas_call(
        functools.partial(_peer_dense_kernel, te=te, np_tiles=np_tiles),
        grid=(n // tt, PEER_EXPERTS // te),
        in_specs=[pl.BlockSpec((tt, D_MODEL), lambda i, e: (i, 0)),
                  pl.BlockSpec((te, D_MODEL), lambda i, e: (e, 0)),
                  pl.BlockSpec((D_MODEL, te), lambda i, e: (0, e)),
                  rspec, rspec, fspec, fspec,
                  pl.BlockSpec((tt, D_MODEL), lambda i, e: (i, 0)),
                  pl.BlockSpec((1, D_MODEL), lambda i, e: (0, 0))],
        out_specs=[pl.BlockSpec((tt, D_MODEL), lambda i, e: (jnp.minimum(i, np_tiles - 1), 0)),
                   pl.BlockSpec((tt, D_MODEL), lambda i, e: (jnp.maximum(i - np_tiles, 0), 0))],
        out_shape=[jax.ShapeDtypeStruct((n_p, D_MODEL), F32),
                   jax.ShapeDtypeStruct((n - n_p, D_MODEL), F32)],
        scratch_shapes=([pltpu.VMEM((D_MODEL, tt), F32)]
                        + [pltpu.VMEM((PEER_SUB, tt), F32)] * nsub
                        + [pltpu.VMEM((PEER_SUB, tt), BF16)] * nsub),
        compiler_params=_cparams(("arbitrary", "arbitrary")),
        name="peer_dense",
    )(xn, u, v_t, e0, ni, e1, rj, x, g_final.reshape(1, D_MODEL))


def _lane_row(vec, start, width=GATE_W):
    return jnp.zeros((1, width), F32).at[0, start:start + vec.shape[0]].set(vec.astype(F32))


def _rep_matrix(first_row, heads, lanes_per_head, rows=GATE_W):
    r = jnp.arange(rows)[:, None]
    c = jnp.arange(heads * lanes_per_head)[None, :]
    return ((c // lanes_per_head) + first_row == r).astype(BF16)


def _pad_tail(buf):
    return jnp.pad(buf, ((0, 0), (TAIL - (CONV_K - 1), 0), (0, 0)))


def kernel(x_prompt, x_sample, state_dn_conv, state_dn, state_ssm_conv, state_ssm, cache_mem_k, cache_mem_v,
           mem_prompt, g_mix, w_in, dn_conv_w, dn_A_log, dn_dt_bias, dn_norm_w, ssm_conv_w, ssm_conv_b,
           ssm_A_log, ssm_dt_bias, ssm_D, ssm_norm_w, w_out, g_xattn, g_mem, w_xq, w_mkv, w_xo, g_ffn, w_pq,
           peer_sub_keys, peer_u, peer_v, g_final):
    depth = g_mix.shape[0]
    assert depth == 1
    bp, lp, d = x_prompt.shape
    bs, ls, _ = x_sample.shape
    n_p, n_s = bp * lp, bs * ls
    n = n_p + n_s
    ncp = lp // CHUNK
    tm = 512
    x = (x_prompt.reshape(n_p, d), x_sample.reshape(n_s, d))

    wi = w_in[0]
    o_xbc = DN_QKV
    o_zdn = o_xbc + SSM_XBC
    o_zssm = o_zdn + DN_WIDTH
    o_small = o_zssm + SSM_WIDTH
    n_small = 2 * DN_HEADS + SSM_HEADS
    w_cat = jnp.concatenate([
        wi[:, :DN_QKV], wi[:, o_xbc:o_xbc + SSM_XBC],
        wi[:, o_small:o_small + n_small], jnp.zeros((d, COL_ZDN - COL_GATE - n_small), F32),
        wi[:, o_zdn:o_zdn + DN_WIDTH], wi[:, o_zssm:o_zssm + SSM_WIDTH]], axis=1).astype(BF16)

    proj = norm_matmul(x, g_mix[0], w_cat, tm=2 * tm, tn=1024)
    proj_c = proj.reshape(n // CHUNK, CHUNK, PROJ_COLS)
    proj_s = proj.reshape(n // ls, ls, PROJ_COLS)

    gdn_params = (dn_conv_w[0],
                  jnp.concatenate([_lane_row(dn_A_log[0], DN_HEADS), _lane_row(dn_dt_bias[0], DN_HEADS)], axis=0),
                  dn_norm_w[0].reshape(1, DN_DK),
                  _rep_matrix(0, DN_HEADS, DN_DK), _rep_matrix(DN_HEADS, DN_HEADS, DN_DK))
    ssd_params = (ssm_conv_w[0], ssm_conv_b[0].reshape(1, SSM_XBC),
                  jnp.concatenate([_lane_row(ssm_A_log[0], 2 * DN_HEADS), _lane_row(ssm_dt_bias[0], 2 * DN_HEADS)],
                                  axis=0),
                  jnp.repeat(ssm_D[0], SSM_HEADDIM).reshape(1, SSM_WIDTH),
                  ssm_norm_w[0].reshape(1, SSM_WIDTH),
                  _rep_matrix(2 * DN_HEADS, SSM_HEADS, SSM_HEADDIM), _rep_matrix(2 * DN_HEADS, SSM_HEADS, LANES))

    odn_p, p_dn = gdn_mixer(proj_c, 0, bp, ncp, CHUNK, 4, 1, gdn_params)
    odn_s, s_dn = gdn_mixer(proj_s, n_p // ls, bs, 1, ls, 1, 8, gdn_params,
                            buf=_pad_tail(state_dn_conv[0]), s0=state_dn[0])
    ossm_p, p_ss = ssd_mixer(proj_c, 0, bp, ncp, CHUNK, 4, 1, ssd_params)
    ossm_s, s_ss = ssd_mixer(proj_s, n_p // ls, bs, 1, ls, 1, 8, ssd_params,
                             buf=_pad_tail(state_ssm_conv[0]),
                             s0=state_ssm[0].reshape(bs, SSM_PAIRS, PAIR_W, SSM_STATE))
    wo = w_out[0].astype(BF16)
    x1 = mixer_out((odn_p.reshape(n_p, DN_WIDTH), odn_s.reshape(n_s, DN_WIDTH)),
                   (ossm_p.reshape(n_p, SSM_WIDTH), ossm_s.reshape(n_s, SSM_WIDTH)),
                   wo[:DN_WIDTH], wo[DN_WIDTH:], x, tm=tm)

    last_c = slice(ncp - 1, bp * ncp, ncp)
    hist_c = slice(CHUNK - (CONV_K - 1), CHUNK)
    hist_s = slice(ls - (CONV_K - 1), ls)
    p_dnc = proj_c[last_c, hist_c, COL_QKV:COL_QKV + DN_QKV]
    p_sc = proj_c[last_c, hist_c, COL_XBC:COL_XBC + SSM_XBC]
    s_dnc = proj_s[n_p // ls:, hist_s, COL_QKV:COL_QKV + DN_QKV]
    s_sc = proj_s[n_p // ls:, hist_s, COL_XBC:COL_XBC + SSM_XBC]

    mkv = norm_matmul(mem_prompt.reshape(bp * MEM_LEN, d), g_mem[0], w_mkv[0].astype(BF16), tm=tm, tn=1024)
    mkv3 = mkv.reshape(bp, MEM_LEN, 2 * d)
    q = norm_matmul(x1, g_xattn[0], w_xq[0].astype(BF16), tm=tm, tn=1024)
    tl = 512
    a_p = xattn_core(q.reshape(n // tl, tl, d), 0, bp, lp // tl, tl, mkv3, mkv3, kv_col=(0, 1))
    a_s = xattn_cache(q.reshape(n // ls, ls, d), n_p // ls, bs, ls, 4, cache_mem_k, cache_mem_v)
    x2 = matmul_res((a_p.reshape(n_p, d), a_s.reshape(n_s, d)), w_xo[0].astype(BF16), x1, tm=tm)

    xn, e0, ni, e1, rj = peer_route(x2, g_ffn[0], jnp.transpose(w_pq[0]).astype(BF16),
                                    peer_sub_keys[0].astype(BF16), tt=512)
    y_p, y_s = peer_dense(xn, peer_u[0].astype(BF16), jnp.transpose(peer_v[0]).astype(BF16),
                          e0, ni, e1, rj, x2, g_final, n_p, tt=512, te=2048)

    def heads(m):
        return m.reshape(bp, MEM_LEN, MEM_HEADS, MEM_HD)[None]

    return (y_p.reshape(bp, lp, d), y_s.reshape(bs, ls, d),
            p_dnc[None], p_dn[None], p_sc[None], p_ss.reshape(bp, SSM_HEADS, SSM_HEADDIM, SSM_STATE)[None],
            heads(mkv3[:, :, :d]), heads(mkv3[:, :, d:]),
            s_dnc[None], s_dn[None], s_sc[None], s_ss.reshape(bs, SSM_HEADS, SSM_HEADDIM, SSM_STATE)[None])
```

```python
import functools

import jax
import jax.numpy as jnp
from jax import lax
from jax.experimental import pallas as pl
from jax.experimental.pallas import tpu as pltpu

F32 = jnp.float32
BF16 = jnp.bfloat16
EPS = 1e-6

D_MODEL = 1024
CONV_K = 4
CHUNK = 64
DN_HEADS = 8
DN_DK = 128
DN_WIDTH = 1024
DN_QKV = 3072
SSM_HEADS = 16
SSM_HEADDIM = 64
SSM_STATE = 128
SSM_WIDTH = 1024
SSM_XBC = 1536
MEM_LEN = 256
MEM_HEADS = 4
MEM_HD = 256
PEER_HEADS = 8
PEER_NKEYS = 128
PEER_EXPERTS = PEER_NKEYS * PEER_NKEYS
PEER_TOPK = 16
PEER_HALF = 128

COL_QKV = 0
COL_XBC = 3072
COL_GATE = 4608
COL_ZDN = 5120
COL_ZSSM = 6144
PROJ_COLS = 7168
GATE_W = 128
TAIL = 8
INV_BLOCK = 16
NOT_TOP = 99.0
LANES = 128
BF16_SUBLANES = 16
PEER_SUB = 1024

VMEM_LIMIT = 56 * 1024 * 1024


def _cparams(sem):
    return pltpu.CompilerParams(dimension_semantics=sem, vmem_limit_bytes=VMEM_LIMIT)


def _bdot(a, b):
    return jnp.dot(a.astype(BF16), b.astype(BF16), preferred_element_type=F32)


def _bdot_nt(a, b):
    return lax.dot_general(a.astype(BF16), b.astype(BF16), (((1,), (1,)), ((), ())),
                           preferred_element_type=F32)


def _bdot_tn(a, b):
    return lax.dot_general(a.astype(BF16), b.astype(BF16), (((0,), (0,)), ((), ())),
                           preferred_element_type=F32)


def _split3(x):
    hi = x.astype(BF16)
    r = x - hi.astype(F32)
    mid = r.astype(BF16)
    lo = (r - mid.astype(F32)).astype(BF16)
    return hi, mid, lo


def _dot_sel_rhs(x, sel):
    hi, mid, lo = _split3(x)
    d = functools.partial(jnp.dot, preferred_element_type=F32)
    return d(hi, sel) + d(mid, sel) + d(lo, sel)


def _dot_sel_lhs(sel, x):
    hi, mid, lo = _split3(x)
    d = functools.partial(jnp.dot, preferred_element_type=F32)
    return d(sel, hi) + d(sel, mid) + d(sel, lo)


def _silu(x):
    return x * jax.nn.sigmoid(x)


def _softplus(x):
    return jnp.maximum(x, 0.0) + jnp.log1p(jnp.exp(-jnp.abs(x)))


def _unit_lower_inverse(lms, n):
    row = lax.broadcasted_iota(jnp.int32, (n, n), 0)
    col = lax.broadcasted_iota(jnp.int32, (n, n), 1)
    eye = (row == col).astype(F32)

    def nilpotent_inverse(xs, index):
        invs = [eye - x for x in xs]
        ps = xs
        k = 2
        while k < index:
            ps = [_bdot(p, p) for p in ps]
            invs = [inv + _bdot(inv, p) for inv, p in zip(invs, ps)]
            k *= 2
        return invs

    if n <= INV_BLOCK:
        return nilpotent_inverse(lms, n)
    shift = INV_BLOCK.bit_length() - 1
    same = jnp.right_shift(row, shift) == jnp.right_shift(col, shift)
    dinvs = nilpotent_inverse([jnp.where(same, lm, 0.0) for lm in lms], INV_BLOCK)
    fs = [_bdot(dinv, jnp.where(same, 0.0, lm)) for dinv, lm in zip(dinvs, lms)]
    finvs = nilpotent_inverse(fs, n // INV_BLOCK)
    return [_bdot(finv, dinv) for finv, dinv in zip(finvs, dinvs)]


def _causal_conv(x, xp_ref, w, lc):
    xp_ref[TAIL:TAIL + lc, :] = x
    y = x * w[3:4, :]
    for s in range(1, CONV_K):
        y = y + xp_ref[TAIL - s:TAIL - s + lc, :] * w[CONV_K - 1 - s:CONV_K - s, :]
    xp_ref[0:TAIL, :] = xp_ref[lc:lc + TAIL, :]
    return y


def _decay_matrix(col, row, lower_incl):
    diff = jnp.where(lower_incl, col - row, 0.0)
    return jnp.where(lower_incl, jnp.exp(diff), 0.0)


def _pair_specs(tm, k, np_tiles, two_axes):
    if two_axes:
        return [pl.BlockSpec((tm, k), lambda i, j: (jnp.minimum(i, np_tiles - 1), 0)),
                pl.BlockSpec((tm, k), lambda i, j: (jnp.maximum(i - np_tiles, 0), 0))]
    return [pl.BlockSpec((tm, k), lambda i: (jnp.minimum(i, np_tiles - 1), 0)),
            pl.BlockSpec((tm, k), lambda i: (jnp.maximum(i - np_tiles, 0), 0))]


def _pick(is_prompt, p_ref, s_ref):
    return jnp.where(is_prompt, p_ref[...], s_ref[...])


def _rmsnorm_rows(x, g):
    ms = jnp.mean(x * x, axis=-1, keepdims=True)
    return x * lax.rsqrt(ms + EPS) * g


def _norm_matmul_kernel(*refs, np_tiles):
    if np_tiles is None:
        x_ref, g_ref, w_ref, o_ref, xn_ref = refs
    else:
        xp_ref, xs_ref, g_ref, w_ref, o_ref, xn_ref = refs
        is_prompt = pl.program_id(0) < np_tiles

    @pl.when(pl.program_id(1) == 0)
    def _():
        x = x_ref[...] if np_tiles is None else _pick(is_prompt, xp_ref, xs_ref)
        xn_ref[...] = _rmsnorm_rows(x, g_ref[...]).astype(BF16)

    o_ref[...] = jnp.dot(xn_ref[...], w_ref[...], preferred_element_type=F32).astype(o_ref.dtype)


def norm_matmul(x, g, w, *, tm, tn, out_dtype=F32):
    pair = isinstance(x, tuple)
    n = sum(a.shape[0] for a in x) if pair else x.shape[0]
    k, m = w.shape
    assert n % tm == 0 and m % tn == 0
    if pair:
        assert x[0].shape[0] % tm == 0
        np_tiles = x[0].shape[0] // tm
        x_specs = _pair_specs(tm, k, np_tiles, True)
        x_args = list(x)
    else:
        np_tiles = None
        x_specs = [pl.BlockSpec((tm, k), lambda i, j: (i, 0))]
        x_args = [x]
    return pl.pallas_call(
        functools.partial(_norm_matmul_kernel, np_tiles=np_tiles),
        grid=(n // tm, m // tn),
        in_specs=x_specs + [pl.BlockSpec((1, k), lambda i, j: (0, 0)),
                            pl.BlockSpec((k, tn), lambda i, j: (0, j))],
        out_specs=pl.BlockSpec((tm, tn), lambda i, j: (i, j)),
        out_shape=jax.ShapeDtypeStruct((n, m), out_dtype),
        scratch_shapes=[pltpu.VMEM((tm, k), BF16)],
        compiler_params=_cparams(("arbitrary", "arbitrary")),
        name="norm_matmul",
    )(*x_args, g.reshape(1, k), w)


def _mixer_out_kernel(dp_ref, ds_ref, sp_ref, ss_ref, w1_ref, w2_ref, xp_ref, xs_ref, o_ref, *, np_tiles):
    is_prompt = pl.program_id(0) < np_tiles
    a1 = _pick(is_prompt, dp_ref, ds_ref).astype(BF16)
    a2 = _pick(is_prompt, sp_ref, ss_ref).astype(BF16)
    acc = jnp.dot(a1, w1_ref[...], preferred_element_type=F32)
    acc = acc + jnp.dot(a2, w2_ref[...], preferred_element_type=F32)
    o_ref[...] = _pick(is_prompt, xp_ref, xs_ref) + acc


def mixer_out(o_dn, o_ssm, w1, w2, x, *, tm):
    n = x[0].shape[0] + x[1].shape[0]
    k = w1.shape[0]
    m = w1.shape[1]
    np_tiles = x[0].shape[0] // tm
    wspec = pl.BlockSpec((k, m), lambda i: (0, 0))
    return pl.pallas_call(
        functools.partial(_mixer_out_kernel, np_tiles=np_tiles),
        grid=(n // tm,),
        in_specs=(_pair_specs(tm, k, np_tiles, False) + _pair_specs(tm, k, np_tiles, False)
                  + [wspec, wspec] + _pair_specs(tm, m, np_tiles, False)),
        out_specs=pl.BlockSpec((tm, m), lambda i: (i, 0)),
        out_shape=jax.ShapeDtypeStruct((n, m), F32),
        compiler_params=_cparams(("arbitrary",)),
        name="mixer_out",
    )(*o_dn, *o_ssm, w1, w2, *x)


def _matmul_res_kernel(ap_ref, as_ref, w_ref, r_ref, o_ref, *, np_tiles):
    a = _pick(pl.program_id(0) < np_tiles, ap_ref, as_ref).astype(BF16)
    o_ref[...] = r_ref[...] + jnp.dot(a, w_ref[...], preferred_element_type=F32)


def matmul_res(a, w, res, *, tm):
    n, m = res.shape
    k = w.shape[0]
    np_tiles = a[0].shape[0] // tm
    return pl.pallas_call(
        functools.partial(_matmul_res_kernel, np_tiles=np_tiles),
        grid=(n // tm,),
        in_specs=_pair_specs(tm, k, np_tiles, False) + [pl.BlockSpec((k, m), lambda i: (0, 0)),
                                                        pl.BlockSpec((tm, m), lambda i: (i, 0))],
        out_specs=pl.BlockSpec((tm, m), lambda i: (i, 0)),
        out_shape=jax.ShapeDtypeStruct((n, m), F32),
        compiler_params=_cparams(("arbitrary",)),
        name="matmul_res",
    )(*a, w, res)


def _gdn_kernel(*refs, lc, ng, nbk, has_state):
    refs = list(refs)

    def take(k):
        out = refs[:k]
        del refs[:k]
        return out

    qkv_refs, gate_refs, z_refs = take(ng), take(ng), take(ng)
    if has_state:
        buf_ref, s0_ref = take(2)
    cw_ref, gp_ref, nw_ref, repb_ref, repg_ref, o_ref, s_ref, xp_ref = take(8)
    nseq = ng * nbk
    where = [divmod(j, nbk) for j in range(nseq)]

    @pl.when(pl.program_id(1) == 0)
    def _():
        if has_state:
            xp_ref[:, 0:TAIL, :] = buf_ref[...]
            s_ref[...] = s0_ref[...]
        else:
            xp_ref[:, 0:TAIL, :] = jnp.zeros((nseq, TAIL, DN_QKV), F32)
            s_ref[...] = jnp.zeros(s_ref.shape, F32)

    row = lax.broadcasted_iota(jnp.int32, (lc, lc), 0)
    col = lax.broadcasted_iota(jnp.int32, (lc, lc), 1)
    incl = row >= col
    strict = row > col
    tri = incl.astype(BF16)

    seqs = []
    for j, (g_, k_) in enumerate(where):
        qkv = _silu(_causal_conv(qkv_refs[g_][k_], xp_ref.at[j], cw_ref[...], lc))
        gate = gate_refs[g_][k_]
        beta_all = jax.nn.sigmoid(gate)
        g_all = -jnp.exp(gp_ref[0:1, :]) * _softplus(gate + gp_ref[1:2, :])
        gc = _dot_sel_lhs(tri, g_all)
        gc_b = _dot_sel_rhs(gc, repg_ref[...])
        gc_last_b = gc_b[lc - 1:lc, :]
        seqs.append(dict(qkv=qkv, gc_t=jnp.transpose(gc), gc_b=gc_b,
                         beta_b=_dot_sel_rhs(beta_all, repb_ref[...]),
                         egc_b=jnp.exp(gc_b), ekd_b=jnp.exp(gc_last_b - gc_b), egt_b=jnp.exp(gc_last_b)))

    units = [(j, h) for j in range(nseq) for h in range(DN_HEADS)]
    qs, ks, vbs, kbegs, decays = [], [], [], [], []
    for j, h in units:
        sq = seqs[j]
        sl = slice(h * DN_DK, (h + 1) * DN_DK)
        qh = sq["qkv"][:, h * DN_DK:(h + 1) * DN_DK]
        kh = sq["qkv"][:, DN_WIDTH + h * DN_DK:DN_WIDTH + (h + 1) * DN_DK]
        vh = sq["qkv"][:, 2 * DN_WIDTH + h * DN_DK:2 * DN_WIDTH + (h + 1) * DN_DK]
        qh = qh * lax.rsqrt(jnp.sum(qh * qh, axis=-1, keepdims=True) + EPS) * (DN_DK ** -0.5)
        kh = kh * lax.rsqrt(jnp.sum(kh * kh, axis=-1, keepdims=True) + EPS)
        bh = sq["beta_b"][:, sl]
        qs.append(qh)
        ks.append(kh)
        vbs.append(vh * bh)
        kbegs.append((kh * bh, sq["egc_b"][:, sl]))
        decays.append(_decay_matrix(sq["gc_b"][:, h * DN_DK:h * DN_DK + lc],
                                    sq["gc_t"][DN_HEADS + h:DN_HEADS + h + 1, :], incl))
    lowers = [jnp.where(strict, _bdot_nt(kb, kh) * dec, 0.0)
              for (kb, _), kh, dec in zip(kbegs, ks, decays)]
    attns = [_bdot_nt(qh, kh) * dec for qh, kh, dec in zip(qs, ks, decays)]
    tinvs = _unit_lower_inverse(lowers, lc)
    us = [_bdot(t, vb) for t, vb in zip(tinvs, vbs)]
    ws = [_bdot(t, kb * eg) for t, (kb, eg) in zip(tinvs, kbegs)]
    states = [s_ref[j, h] for j, h in units]
    v_news = [u - _bdot(w, s) for u, w, s in zip(us, ws, states)]
    os_ = [_bdot(qh * eg, s) + _bdot(a, vn)
           for qh, (_, eg), s, a, vn in zip(qs, kbegs, states, attns, v_news)]
    for (j, h), kh, s, vn, o in zip(units, ks, states, v_news, os_):
        g_, k_ = where[j]
        sl = slice(h * DN_DK, (h + 1) * DN_DK)
        sq = seqs[j]
        s_ref[j, h] = s * sq["egt_b"][:, sl] + _bdot_tn(kh * sq["ekd_b"][:, sl], vn)
        o = o * lax.rsqrt(jnp.mean(o * o, axis=-1, keepdims=True) + EPS) * nw_ref[...]
        o_ref[j, 0, :, sl] = o * _silu(z_refs[g_][k_, :, sl])


def gdn_mixer(proj3, row0, nseq, nchunk, lc, ng, nbk, params, buf=None, s0=None):
    has_state = buf is not None
    step = ng * nbk
    assert nseq % step == 0 and row0 % nbk == 0 and (nbk == 1 or nchunk == 1)

    def proj_specs(width, col_block):
        return [pl.BlockSpec((nbk, lc, width),
                             lambda b, c, g_=g_: (row0 // nbk + (b * ng + g_) * nchunk + c, 0, col_block))
                for g_ in range(ng)]

    in_specs = (proj_specs(DN_QKV, COL_QKV // DN_QKV) + proj_specs(GATE_W, COL_GATE // GATE_W)
                + proj_specs(DN_WIDTH, COL_ZDN // DN_WIDTH))
    args = [proj3] * (3 * ng)
    state_spec = pl.BlockSpec((step, DN_HEADS, DN_DK, DN_DK), lambda b, c: (b, 0, 0, 0))
    if has_state:
        in_specs += [pl.BlockSpec((step, TAIL, DN_QKV), lambda b, c: (b, 0, 0)), state_spec]
        args += [buf, s0]
    for prm in params:
        in_specs.append(pl.BlockSpec(prm.shape, lambda b, c: (0, 0)))
        args.append(prm)
    return pl.pallas_call(
        functools.partial(_gdn_kernel, lc=lc, ng=ng, nbk=nbk, has_state=has_state),
        grid=(nseq // step, nchunk),
        in_specs=in_specs,
        out_specs=[pl.BlockSpec((step, 1, lc, DN_WIDTH), lambda b, c: (b, c, 0, 0)), state_spec],
        out_shape=[jax.ShapeDtypeStruct((nseq, nchunk, lc, DN_WIDTH), F32),
                   jax.ShapeDtypeStruct((nseq, DN_HEADS, DN_DK, DN_DK), F32)],
        scratch_shapes=[pltpu.VMEM((step, TAIL + lc, DN_QKV), F32)],
        compiler_params=_cparams(("arbitrary", "arbitrary")),
        name="gdn_mixer_state" if has_state else "gdn_mixer",
    )(*args)


SSM_PAIRS = SSM_HEADS // 2
PAIR_W = 2 * SSM_HEADDIM
GROUP_W = SSM_WIDTH // 2


def _ssd_kernel(*refs, lc, ng, nbk, has_state):
    refs = list(refs)

    def take(k):
        out = refs[:k]
        del refs[:k]
        return out

    xbc_refs, gate_refs, z_refs = take(ng), take(ng), take(ng)
    if has_state:
        buf_ref, s0_ref = take(2)
    (cw_ref, cb_ref, gp_ref, dd_ref, nw_ref, reps_ref, repw_ref, o_ref, s_ref, xp_ref) = take(10)
    nseq = ng * nbk
    where = [divmod(j, nbk) for j in range(nseq)]

    @pl.when(pl.program_id(1) == 0)
    def _():
        if has_state:
            xp_ref[:, 0:TAIL, :] = buf_ref[...]
            s_ref[...] = s0_ref[...]
        else:
            xp_ref[:, 0:TAIL, :] = jnp.zeros((nseq, TAIL, SSM_XBC), F32)
            s_ref[...] = jnp.zeros(s_ref.shape, F32)

    row = lax.broadcasted_iota(jnp.int32, (lc, lc), 0)
    col = lax.broadcasted_iota(jnp.int32, (lc, lc), 1)
    incl = row >= col
    tri = incl.astype(BF16)
    first_head = lax.broadcasted_iota(jnp.int32, (lc, PAIR_W), 1) < SSM_HEADDIM

    seqs = []
    for j, (g_, k_) in enumerate(where):
        xbc = _silu(_causal_conv(xbc_refs[g_][k_], xp_ref.at[j], cw_ref[...], lc) + cb_ref[...])
        xs = xbc[:, :SSM_WIDTH]
        dt = _softplus(gate_refs[g_][k_] + gp_ref[1:2, :])
        acs = _dot_sel_lhs(tri, dt * (-jnp.exp(gp_ref[0:1, :])))
        acs_b = _dot_sel_rhs(acs, reps_ref[...])
        acs_w = _dot_sel_rhs(acs, repw_ref[...])
        seqs.append(dict(xbc=xbc, xs=xs, acs_t=jnp.transpose(acs), acs_w=acs_w,
                         xd=xs * _dot_sel_rhs(dt, reps_ref[...]),
                         eacs_b=jnp.exp(acs_b), ends_b=jnp.exp(acs_b[lc - 1:lc, :] - acs_b),
                         ecd_w=jnp.exp(acs_w[lc - 1:lc, :])))

    def bmat(sq, g):
        return sq["xbc"][:, SSM_WIDTH + g * SSM_STATE:SSM_WIDTH + (g + 1) * SSM_STATE]

    def cmat(sq, g):
        o = SSM_WIDTH + 2 * SSM_STATE
        return sq["xbc"][:, o + g * SSM_STATE:o + (g + 1) * SSM_STATE]

    cbs = [[_bdot_nt(cmat(sq, g), bmat(sq, g)) for g in range(2)] for sq in seqs]
    units = [(j, p) for j in range(nseq) for p in range(SSM_PAIRS)]
    lmats = [[_decay_matrix(seqs[j]["acs_w"][:, hh * LANES:hh * LANES + lc],
                            seqs[j]["acs_t"][2 * DN_HEADS + hh:2 * DN_HEADS + hh + 1, :], incl)
              for hh in (2 * p, 2 * p + 1)] for j, p in units]
    prevs = [s_ref[j, p] for j, p in units]
    y_diag, y_off, sts = [], [], []
    for (j, p), lm, prev in zip(units, lmats, prevs):
        sq = seqs[j]
        g = p // (SSM_PAIRS // 2)
        psl = slice(p * PAIR_W, (p + 1) * PAIR_W)
        xd_p = sq["xd"][:, psl]
        y_diag.append([_bdot(cbs[j][g] * lm[0], xd_p), _bdot(cbs[j][g] * lm[1], xd_p)])
        y_off.append(_bdot_nt(cmat(sq, g), prev))
        sts.append(_bdot_tn(xd_p * sq["ends_b"][:, psl], bmat(sq, g)))
    ys = []
    ssq = [[None, None] for _ in range(nseq)]
    for (j, p), yd, yo, st, prev in zip(units, y_diag, y_off, sts, prevs):
        sq = seqs[j]
        g_, k_ = where[j]
        g = p // (SSM_PAIRS // 2)
        psl = slice(p * PAIR_W, (p + 1) * PAIR_W)
        cd = jnp.concatenate(
            [jnp.broadcast_to(sq["ecd_w"][:, (2 * p) * LANES:(2 * p + 1) * LANES], (SSM_HEADDIM, SSM_STATE)),
             jnp.broadcast_to(sq["ecd_w"][:, (2 * p + 1) * LANES:(2 * p + 2) * LANES], (SSM_HEADDIM, SSM_STATE))],
            axis=0)
        s_ref[j, p] = prev * cd + st
        y = jnp.where(first_head, yd[0], yd[1]) + yo * sq["eacs_b"][:, psl] + dd_ref[:, psl] * sq["xs"][:, psl]
        y = y * _silu(z_refs[g_][k_, :, psl])
        s2 = jnp.sum(y * y, axis=-1, keepdims=True)
        ssq[j][g] = s2 if ssq[j][g] is None else ssq[j][g] + s2
        ys.append(y)
    for (j, p), y in zip(units, ys):
        g = p // (SSM_PAIRS // 2)
        psl = slice(p * PAIR_W, (p + 1) * PAIR_W)
        o_ref[j, 0, :, psl] = y * lax.rsqrt(ssq[j][g] * (1.0 / GROUP_W) + EPS) * nw_ref[:, psl]


def ssd_mixer(proj3, row0, nseq, nchunk, lc, ng, nbk, params, buf=None, s0=None):
    has_state = buf is not None
    step = ng * nbk
    assert nseq % step == 0 and row0 % nbk == 0 and (nbk == 1 or nchunk == 1)

    def proj_specs(width, col_block):
        return [pl.BlockSpec((nbk, lc, width),
                             lambda b, c, g_=g_: (row0 // nbk + (b * ng + g_) * nchunk + c, 0, col_block))
                for g_ in range(ng)]

    in_specs = (proj_specs(SSM_XBC, COL_XBC // SSM_XBC) + proj_specs(GATE_W, COL_GATE // GATE_W)
                + proj_specs(SSM_WIDTH, COL_ZSSM // SSM_WIDTH))
    args = [proj3] * (3 * ng)
    state_spec = pl.BlockSpec((step, SSM_PAIRS, PAIR_W, SSM_STATE), lambda b, c: (b, 0, 0, 0))
    if has_state:
        in_specs += [pl.BlockSpec((step, TAIL, SSM_XBC), lambda b, c: (b, 0, 0)), state_spec]
        args += [buf, s0]
    for prm in params:
        in_specs.append(pl.BlockSpec(prm.shape, lambda b, c: (0, 0)))
        args.append(prm)
    return pl.pallas_call(
        functools.partial(_ssd_kernel, lc=lc, ng=ng, nbk=nbk, has_state=has_state),
        grid=(nseq // step, nchunk),
        in_specs=in_specs,
        out_specs=[pl.BlockSpec((step, 1, lc, SSM_WIDTH), lambda b, c: (b, c, 0, 0)), state_spec],
        out_shape=[jax.ShapeDtypeStruct((nseq, nchunk, lc, SSM_WIDTH), F32),
                   jax.ShapeDtypeStruct((nseq, SSM_PAIRS, PAIR_W, SSM_STATE), F32)],
        scratch_shapes=[pltpu.VMEM((step, TAIL + lc, SSM_XBC), F32)],
        compiler_params=_cparams(("arbitrary", "arbitrary")),
        name="ssd_mixer_state" if has_state else "ssd_mixer",
    )(*args)


def _xattn_kernel(q_ref, k_ref, v_ref, o_ref):
    for h in range(MEM_HEADS):
        sl = slice(h * MEM_HD, (h + 1) * MEM_HD)
        k = k_ref[0, :, sl]
        v = v_ref[0, :, sl]
        s = _bdot_nt(q_ref[0, :, sl], k) * (MEM_HD ** -0.5)
        s = s - jnp.max(s, axis=-1, keepdims=True)
        p = jnp.exp(s)
        p = p / jnp.sum(p, axis=-1, keepdims=True)
        o_ref[0, :, sl] = _bdot(p, v)


def _xattn_cache_kernel(q_ref, k_ref, v_ref, o_ref, *, tl, nb):
    shape = (MEM_HEADS * tl, MEM_LEN * MEM_HEADS)
    row_head = jnp.right_shift(lax.broadcasted_iota(jnp.int32, shape, 0), tl.bit_length() - 1)
    col_head = jnp.bitwise_and(lax.broadcasted_iota(jnp.int32, shape, 1), MEM_HEADS - 1)
    own_head = row_head == col_head
    for j in range(nb):
        k2 = k_ref[j].reshape(MEM_LEN * MEM_HEADS, MEM_HD)
        v2 = v_ref[j].reshape(MEM_LEN * MEM_HEADS, MEM_HD)
        q4 = jnp.concatenate([q_ref[j, :, h * MEM_HD:(h + 1) * MEM_HD] for h in range(MEM_HEADS)], axis=0)
        s = jnp.where(own_head, _bdot_nt(q4, k2) * (MEM_HD ** -0.5), -jnp.inf)
        s = s - jnp.max(s, axis=-1, keepdims=True)
        p = jnp.exp(s)
        p = p / jnp.sum(p, axis=-1, keepdims=True)
        o4 = _bdot(p, v2)
        for h in range(MEM_HEADS):
            o_ref[j, :, h * MEM_HD:(h + 1) * MEM_HD] = o4[h * tl:(h + 1) * tl, :]


def xattn_cache(q3, row0, nseq, tl, nb, cache_k, cache_v):
    assert tl & (tl - 1) == 0 and MEM_HEADS & (MEM_HEADS - 1) == 0 and nseq % nb == 0 and row0 % nb == 0
    kv_spec = pl.BlockSpec((None, nb, MEM_LEN, MEM_HEADS, MEM_HD), lambda b: (0, b, 0, 0, 0))
    return pl.pallas_call(
        functools.partial(_xattn_cache_kernel, tl=tl, nb=nb),
        grid=(nseq // nb,),
        in_specs=[pl.BlockSpec((nb, tl, D_MODEL), lambda b: (row0 // nb + b, 0, 0)), kv_spec, kv_spec],
        out_specs=pl.BlockSpec((nb, tl, D_MODEL), lambda b: (b, 0, 0)),
        out_shape=jax.ShapeDtypeStruct((nseq, tl, D_MODEL), F32),
        compiler_params=_cparams(("arbitrary",)),
        name="xattn_cache",
    )(q3, cache_k, cache_v)


def xattn_core(q3, row0, nseq, ntile, tl, mem_k, mem_v, kv_col=(0, 0)):
    kv_specs = [pl.BlockSpec((1, MEM_LEN, D_MODEL), lambda b, t, c_=c_: (b, 0, c_)) for c_ in kv_col]
    return pl.pallas_call(
        _xattn_kernel,
        grid=(nseq, ntile),
        in_specs=[pl.BlockSpec((1, tl, D_MODEL), lambda b, t: (row0 + b * ntile + t, 0, 0))] + kv_specs,
        out_specs=pl.BlockSpec((1, tl, D_MODEL), lambda b, t: (b * ntile + t, 0, 0)),
        out_shape=jax.ShapeDtypeStruct((nseq * ntile, tl, D_MODEL), F32),
        compiler_params=_cparams(("arbitrary", "arbitrary")),
        name="xattn_core",
    )(q3, mem_k, mem_v)


MAGNITUDE_BITS = 0x7FFFFFFF
NEG_INF_KEY = -0x800000 ^ MAGNITUDE_BITS
RANK0_MARK = NEG_INF_KEY - 1


def _order_key(x, inverse=False):
    b = x if inverse else lax.bitcast_convert_type(x, jnp.int32)
    k = jnp.where(b < 0, jnp.bitwise_xor(b, MAGNITUDE_BITS), b)
    return lax.bitcast_convert_type(k, F32) if inverse else k


def _top16_rows(s, break_ties):
    n = s.shape[0]
    v = _order_key(s + 0.0)
    vals = []
    for a in range(PEER_TOPK):
        m = jnp.max(v, axis=0, keepdims=True)
        hit = v == m
        if break_ties:
            iota = lax.broadcasted_iota(jnp.int32, s.shape, 0)
            hit = iota == jnp.min(jnp.where(hit, iota, n), axis=0, keepdims=True)
        v = jnp.where(hit, RANK0_MARK - a, v)
        vals.append(m)
    rank = jnp.where(v <= RANK0_MARK, (RANK0_MARK - v).astype(F32), NOT_TOP)
    return _order_key(jnp.concatenate(vals, axis=0), inverse=True), rank


def _pair_top16(s0v, s1v):
    t = s0v.shape[1]
    iota = lax.broadcasted_iota(jnp.int32, (PEER_TOPK, t), 0).astype(F32)
    n = jnp.zeros((PEER_TOPK, t), F32)
    front = s0v + s1v[0:1, :]
    top = s0v[0:1, :] + s1v[0:1, :]
    z = jnp.zeros((1, t), F32)
    for _ in range(PEER_TOPK):
        m = jnp.max(front, axis=0, keepdims=True)
        a_star = jnp.min(jnp.where(front == m, iota, float(PEER_TOPK)), axis=0, keepdims=True)
        hit = iota == a_star
        z = z + jnp.exp(m - top)
        n = jnp.where(hit, n + 1.0, n)
        nxt = jnp.full((PEER_TOPK, t), -jnp.inf, F32)
        for b in range(1, PEER_TOPK):
            nxt = jnp.where(n == float(b), s1v[b:b + 1, :], nxt)
        front = jnp.where(hit, s0v + nxt, front)
    return n, z


def _peer_route_kernel(x_ref, g_ref, wq_ref, sk_ref, xn_ref, e0_ref, ni_ref, e1_ref, rj_ref, q_ref):
    xn = _rmsnorm_rows(x_ref[...], g_ref[...]).astype(BF16)
    xn_ref[...] = xn
    q_ref[...] = lax.dot_general(wq_ref[...], xn, (((1,), (1,)), ((), ())),
                                 preferred_element_type=F32).astype(BF16)
    tt = x_ref.shape[0]

    def route(break_ties):
        def per_head(h, bad):
            for lb in range(tt // LANES):
                ls = slice(lb * LANES, (lb + 1) * LANES)

                def ranked_half(c):
                    r0 = pl.multiple_of((2 * h + c) * PEER_HALF, PEER_HALF)
                    s = jnp.dot(sk_ref[c], q_ref[pl.ds(r0, PEER_HALF), ls],
                                preferred_element_type=F32)
                    vals, rank = _top16_rows(s, break_ties)
                    inside = rank < float(PEER_TOPK)
                    ex = jnp.where(inside, jnp.exp(jnp.where(inside, s - vals[0:1, :], 0.0)), 0.0)
                    count = jnp.sum(jnp.where(inside, 1.0, 0.0), axis=0, keepdims=True)
                    return vals, rank, ex, jnp.abs(count - float(PEER_TOPK))

                s1v, rank1, e1, off1 = ranked_half(1)
                e1_ref[h, :, ls] = e1.astype(BF16)
                rj_ref[h, :, ls] = rank1.astype(BF16)
                s0v, rank0, e0, off0 = ranked_half(0)
                if not break_ties:
                    bad = jnp.maximum(bad, jnp.maximum(off0, off1))
                n, z = _pair_top16(s0v, s1v)
                e0_ref[h, :, ls] = e0 / z
                ni = jnp.zeros_like(rank0)
                for a in range(PEER_TOPK):
                    ni = jnp.where(rank0 == float(a), n[a:a + 1, :], ni)
                ni_ref[h, :, ls] = ni
            return bad

        return lax.fori_loop(0, PEER_HEADS, per_head, jnp.zeros((1, LANES), F32))

    has_ties = jnp.max(route(break_ties=False)) > 0.0

    @pl.when(has_ties)
    def _():
        route(break_ties=True)


def peer_route(x, g, wq_t, sub_keys, *, tt):
    n = x.shape[0]
    fspec = pl.BlockSpec((PEER_HEADS, PEER_NKEYS, tt), lambda i: (0, 0, i))

    def fac(dtype):
        return jax.ShapeDtypeStruct((PEER_HEADS, PEER_NKEYS, n), dtype)

    return pl.pallas_call(
        _peer_route_kernel,
        grid=(n // tt,),
        in_specs=[pl.BlockSpec((tt, D_MODEL), lambda i: (i, 0)),
                  pl.BlockSpec((1, D_MODEL), lambda i: (0, 0)),
                  pl.BlockSpec(wq_t.shape, lambda i: (0, 0)),
                  pl.BlockSpec(sub_keys.shape, lambda i: (0, 0, 0))],
        out_specs=[pl.BlockSpec((tt, D_MODEL), lambda i: (i, 0)), fspec, fspec, fspec, fspec],
        out_shape=[jax.ShapeDtypeStruct((n, D_MODEL), BF16), fac(F32), fac(F32), fac(BF16), fac(BF16)],
        scratch_shapes=[pltpu.VMEM((PEER_HEADS * 2 * PEER_HALF, tt), BF16)],
        compiler_params=_cparams(("arbitrary",)),
        name="peer_route",
    )(x, g.reshape(1, D_MODEL), wq_t, sub_keys)


def _peer_dense_kernel(xn_ref, u_ref, vt_ref, e0_ref, ni_ref, e1_ref, rj_ref, x_ref, gf_ref,
                       yp_ref, ys_ref, acc_ref, *sub_refs, te, np_tiles):
    i_tok = pl.program_id(0)
    e = pl.program_id(1)
    nsub = te // PEER_SUB
    a_refs, p_refs = sub_refs[:nsub], sub_refs[nsub:]
    tt = xn_ref.shape[0]
    rows_per_sub = PEER_SUB // PEER_NKEYS
    zero = jnp.zeros((), BF16)

    @pl.when(e == 0)
    def _():
        acc_ref[...] = jnp.zeros_like(acc_ref)

    def all_rows(row):
        tile = jnp.broadcast_to(row, (BF16_SUBLANES, tt)).astype(BF16)
        return jnp.concatenate([tile] * (PEER_NKEYS // BF16_SUBLANES), axis=0)

    def pre_activations(s):
        rs = slice(s * PEER_SUB, (s + 1) * PEER_SUB)
        a_refs[s][...] = lax.dot_general(u_ref[rs, :], xn_ref[...], (((1,), (1,)), ((), ())),
                                         preferred_element_type=F32)

    def weighted_activations(s):
        for k in range(rows_per_sub):
            r = s * rows_per_sub + k
            rr = slice(k * PEER_NKEYS, (k + 1) * PEER_NKEYS)
            w = None
            for h in range(PEER_HEADS):
                wh = jnp.where(rj_ref[h] < all_rows(ni_ref[h, r:r + 1, :]),
                               all_rows(e0_ref[h, r:r + 1, :]) * e1_ref[h], zero)
                w = wh if w is None else w + wh
            a = a_refs[s][rr, :].astype(BF16)
            act = 0.5 * a * (1.0 + lax.erf(a * (2.0 ** -0.5)))
            p_refs[s][rr, :] = w * act

    pre_activations(0)
    for s in range(nsub):
        if s + 1 < nsub:
            pre_activations(s + 1)
        weighted_activations(s)
        rs = slice(s * PEER_SUB, (s + 1) * PEER_SUB)
        acc_ref[...] += jnp.dot(vt_ref[:, rs], p_refs[s][...], preferred_element_type=F32)

    @pl.when(e == pl.num_programs(1) - 1)
    def _():
        y = _rmsnorm_rows(x_ref[...] + jnp.transpose(acc_ref[...]), gf_ref[...])

        @pl.when(i_tok < np_tiles)
        def _():
            yp_ref[...] = y

        @pl.when(i_tok >= np_tiles)
        def _():
            ys_ref[...] = y


def peer_dense(xn, u, v_t, e0, ni, e1, rj, x, g_final, n_p, *, tt, te):
    n = xn.shape[0]
    np_tiles = n_p // tt
    nsub = te // PEER_SUB
    fspec = pl.BlockSpec((PEER_HEADS, PEER_NKEYS, tt), lambda i, e: (0, 0, i))
    rspec = pl.BlockSpec((PEER_HEADS, te // PEER_NKEYS, tt), lambda i, e: (0, e, i))
    return pl.pallas_call(
        functools.partial(_peer_dense_kernel, te=te, np_tiles=np_tiles),
        grid=(n // tt, PEER_EXPERTS // te),
        in_specs=[pl.BlockSpec((tt, D_MODEL), lambda i, e: (i, 0)),
                  pl.BlockSpec((te, D_MODEL), lambda i, e: (e, 0)),
                  pl.BlockSpec((D_MODEL, te), lambda i, e: (0, e)),
                  rspec, rspec, fspec, fspec,
                  pl.BlockSpec((tt, D_MODEL), lambda i, e: (i, 0)),
                  pl.BlockSpec((1, D_MODEL), lambda i, e: (0, 0))],
        out_specs=[pl.BlockSpec((tt, D_MODEL), lambda i, e: (jnp.minimum(i, np_tiles - 1), 0)),
                   pl.BlockSpec((tt, D_MODEL), lambda i, e: (jnp.maximum(i - np_tiles, 0), 0))],
        out_shape=[jax.ShapeDtypeStruct((n_p, D_MODEL), F32),
                   jax.ShapeDtypeStruct((n - n_p, D_MODEL), F32)],
        scratch_shapes=([pltpu.VMEM((D_MODEL, tt), F32)]
                        + [pltpu.VMEM((PEER_SUB, tt), F32)] * nsub
                        + [pltpu.VMEM((PEER_SUB, tt), BF16)] * nsub),
        compiler_params=_cparams(("arbitrary", "arbitrary")),
        name="peer_dense",
    )(xn, u, v_t, e0, ni, e1, rj, x, g_final.reshape(1, D_MODEL))


def _lane_row(vec, start, width=GATE_W):
    return jnp.zeros((1, width), F32).at[0, start:start + vec.shape[0]].set(vec.astype(F32))


def _rep_matrix(first_row, heads, lanes_per_head, rows=GATE_W):
    r = jnp.arange(rows)[:, None]
    c = jnp.arange(heads * lanes_per_head)[None, :]
    return ((c // lanes_per_head) + first_row == r).astype(BF16)


def _pad_tail(buf):
    return jnp.pad(buf, ((0, 0), (TAIL - (CONV_K - 1), 0), (0, 0)))


def kernel(x_prompt, x_sample, state_dn_conv, state_dn, state_ssm_conv, state_ssm, cache_mem_k, cache_mem_v,
           mem_prompt, g_mix, w_in, dn_conv_w, dn_A_log, dn_dt_bias, dn_norm_w, ssm_conv_w, ssm_conv_b,
           ssm_A_log, ssm_dt_bias, ssm_D, ssm_norm_w, w_out, g_xattn, g_mem, w_xq, w_mkv, w_xo, g_ffn, w_pq,
           peer_sub_keys, peer_u, peer_v, g_final):
    depth = g_mix.shape[0]
    assert depth == 1
    bp, lp, d = x_prompt.shape
    bs, ls, _ = x_sample.shape
    n_p, n_s = bp * lp, bs * ls
    n = n_p + n_s
    ncp = lp // CHUNK
    tm = 512
    x = (x_prompt.reshape(n_p, d), x_sample.reshape(n_s, d))

    wi = w_in[0]
    o_xbc = DN_QKV
    o_zdn = o_xbc + SSM_XBC
    o_zssm = o_zdn + DN_WIDTH
    o_small = o_zssm + SSM_WIDTH
    n_small = 2 * DN_HEADS + SSM_HEADS
    w_cat = jnp.concatenate([
        wi[:, :DN_QKV], wi[:, o_xbc:o_xbc + SSM_XBC],
        wi[:, o_small:o_small + n_small], jnp.zeros((d, COL_ZDN - COL_GATE - n_small), F32),
        wi[:, o_zdn:o_zdn + DN_WIDTH], wi[:, o_zssm:o_zssm + SSM_WIDTH]], axis=1).astype(BF16)

    proj = norm_matmul(x, g_mix[0], w_cat, tm=2 * tm, tn=1024)
    proj_c = proj.reshape(n // CHUNK, CHUNK, PROJ_COLS)
    proj_s = proj.reshape(n // ls, ls, PROJ_COLS)

    gdn_params = (dn_conv_w[0],
                  jnp.concatenate([_lane_row(dn_A_log[0], DN_HEADS), _lane_row(dn_dt_bias[0], DN_HEADS)], axis=0),
                  dn_norm_w[0].reshape(1, DN_DK),
                  _rep_matrix(0, DN_HEADS, DN_DK), _rep_matrix(DN_HEADS, DN_HEADS, DN_DK))
    ssd_params = (ssm_conv_w[0], ssm_conv_b[0].reshape(1, SSM_XBC),
                  jnp.concatenate([_lane_row(ssm_A_log[0], 2 * DN_HEADS), _lane_row(ssm_dt_bias[0], 2 * DN_HEADS)],
                                  axis=0),
                  jnp.repeat(ssm_D[0], SSM_HEADDIM).reshape(1, SSM_WIDTH),
                  ssm_norm_w[0].reshape(1, SSM_WIDTH),
                  _rep_matrix(2 * DN_HEADS, SSM_HEADS, SSM_HEADDIM), _rep_matrix(2 * DN_HEADS, SSM_HEADS, LANES))

    odn_p, p_dn = gdn_mixer(proj_c, 0, bp, ncp, CHUNK, 4, 1, gdn_params)
    odn_s, s_dn = gdn_mixer(proj_s, n_p // ls, bs, 1, ls, 1, 8, gdn_params,
                            buf=_pad_tail(state_dn_conv[0]), s0=state_dn[0])
    ossm_p, p_ss = ssd_mixer(proj_c, 0, bp, ncp, CHUNK, 4, 1, ssd_params)
    ossm_s, s_ss = ssd_mixer(proj_s, n_p // ls, bs, 1, ls, 1, 8, ssd_params,
                             buf=_pad_tail(state_ssm_conv[0]),
                             s0=state_ssm[0].reshape(bs, SSM_PAIRS, PAIR_W, SSM_STATE))
    wo = w_out[0].astype(BF16)
    x1 = mixer_out((odn_p.reshape(n_p, DN_WIDTH), odn_s.reshape(n_s, DN_WIDTH)),
                   (ossm_p.reshape(n_p, SSM_WIDTH), ossm_s.reshape(n_s, SSM_WIDTH)),
                   wo[:DN_WIDTH], wo[DN_WIDTH:], x, tm=tm)

    last_c = slice(ncp - 1, bp * ncp, ncp)
    hist_c = slice(CHUNK - (CONV_K - 1), CHUNK)
    hist_s = slice(ls - (CONV_K - 1), ls)
    p_dnc = proj_c[last_c, hist_c, COL_QKV:COL_QKV + DN_QKV]
    p_sc = proj_c[last_c, hist_c, COL_XBC:COL_XBC + SSM_XBC]
    s_dnc = proj_s[n_p // ls:, hist_s, COL_QKV:COL_QKV + DN_QKV]
    s_sc = proj_s[n_p // ls:, hist_s, COL_XBC:COL_XBC + SSM_XBC]

    mkv = norm_matmul(mem_prompt.reshape(bp * MEM_LEN, d), g_mem[0], w_mkv[0].astype(BF16), tm=tm, tn=1024)
    mkv3 = mkv.reshape(bp, MEM_LEN, 2 * d)
    q = norm_matmul(x1, g_xattn[0], w_xq[0].astype(BF16), tm=tm, tn=1024)
    tl = 512
    a_p = xattn_core(q.reshape(n // tl, tl, d), 0, bp, lp // tl, tl, mkv3, mkv3, kv_col=(0, 1))
    a_s = xattn_cache(q.reshape(n // ls, ls, d), n_p // ls, bs, ls, 4, cache_mem_k, cache_mem_v)
    x2 = matmul_res((a_p.reshape(n_p, d), a_s.reshape(n_s, d)), w_xo[0].astype(BF16), x1, tm=tm)

    xn, e0, ni, e1, rj = peer_route(x2, g_ffn[0], jnp.transpose(w_pq[0]).astype(BF16),
                                    peer_sub_keys[0].astype(BF16), tt=256)
    y_p, y_s = peer_dense(xn, peer_u[0].astype(BF16), jnp.transpose(peer_v[0]).astype(BF16),
                          e0, ni, e1, rj, x2, g_final, n_p, tt=512, te=2048)

    def heads(m):
        return m.reshape(bp, MEM_LEN, MEM_HEADS, MEM_HD)[None]

    return (y_p.reshape(bp, lp, d), y_s.reshape(bs, ls, d),
            p_dnc[None], p_dn[None], p_sc[None], p_ss.reshape(bp, SSM_HEADS, SSM_HEADDIM, SSM_STATE)[None],
            heads(mkv3[:, :, :d]), heads(mkv3[:, :, d:]),
            s_dnc[None], s_dn[None], s_sc[None], s_ss.reshape(bs, SSM_HEADS, SSM_HEADDIM, SSM_STATE)[None])
```
